```python
import jax, jax.numpy as jnp
from jax import lax
import numpy as np

D_MODEL = 1024
BATCH = 8
SEQ = 2048
DEPTH = 1

MIX_WIDTH = D_MODEL
RET_HEADS = 8
RET_WIDTH = MIX_WIDTH // 2
RET_V_DIM = RET_WIDTH // RET_HEADS
RET_QK_DIM = RET_V_DIM // 2
RET_CHUNK = 128
RET_ROT_BASE = 10000.0
ATT_HEADS = 8
ATT_WIDTH = MIX_WIDTH - RET_WIDTH
ATT_HEAD_DIM = ATT_WIDTH // ATT_HEADS
DILATED_PATTERNS = ((128, 1), (512, 4), (2048, 16))
ATT_BLOCK = 128
ROPE_THETA = 500000.0
ROPE_DIM = ATT_HEAD_DIM // 4
D_FF = -(-8 * D_MODEL // (3 * 256)) * 256
RMS_EPS = 1e-6
GN_EPS = 1e-5
PROJ_SPLITS = (RET_HEADS * RET_QK_DIM, RET_HEADS * RET_QK_DIM, RET_WIDTH, RET_WIDTH,
               ATT_WIDTH, ATT_WIDTH, ATT_WIDTH)
PROJ_WIDTH = sum(PROJ_SPLITS)

kernel_name = 'hybrid_retention_dilated_attn_block'


def rms_norm(x, g):
    xf = x.astype(jnp.float32)
    y = xf * lax.rsqrt(jnp.mean(xf * xf, axis=-1, keepdims=True) + RMS_EPS)
    return (y * g.astype(jnp.float32)).astype(x.dtype)


def apply_rotary(x, pos, inv_freq):
    rot = 2 * inv_freq.shape[0]
    ang = pos.astype(jnp.float32)[:, :, None] * inv_freq[None, None, :]
    cos = jnp.cos(ang)[:, :, None, :]
    sin = jnp.sin(ang)[:, :, None, :]
    xr = x[..., :rot].astype(jnp.float32)
    x1, x2 = xr[..., :rot // 2], xr[..., rot // 2:]
    out = jnp.concatenate([x1 * cos - x2 * sin, x1 * sin + x2 * cos], axis=-1).astype(x.dtype)
    return jnp.concatenate([out, x[..., rot:]], axis=-1)


def retention_branch(q, k, v, g, pos):
    B, S, H, dk = q.shape
    dv = v.shape[-1]
    inv_freq = 1.0 / (RET_ROT_BASE ** jnp.linspace(0.0, 1.0, dk // 2, dtype=jnp.float32))
    q = apply_rotary(q, pos, inv_freq)
    k = apply_rotary(k, pos, inv_freq) * (dk ** -0.5)
    C = RET_CHUNK
    N = S // C
    log_g = jnp.log1p(-jnp.exp2(-5.0 - jnp.arange(H, dtype=jnp.float32)))
    idx = jnp.arange(C, dtype=jnp.float32)
    diff = idx[:, None] - idx[None, :]
    inner_decay = jnp.where(diff >= 0, jnp.exp(log_g[:, None, None] * jnp.maximum(diff, 0.0)), 0.0)
    q_decay = jnp.exp(log_g[:, None] * (idx + 1.0)[None, :])
    k_decay = jnp.exp(log_g[:, None] * (C - 1.0 - idx)[None, :])
    chunk_decay = jnp.exp(log_g * C)
    qc = q.reshape(B, N, C, H, dk)
    kc = k.reshape(B, N, C, H, dk)
    vc = v.reshape(B, N, C, H, dv)
    scores = jnp.einsum('bnchd,bnmhd->bnhcm', qc, kc) * inner_decay
    inner = jnp.einsum('bnhcm,bnmhe->bnche', scores, vc)
    kv = jnp.einsum('bnchd,hc,bnche->nbhde', kc, k_decay, vc)

    def step(state, kv_i):
        return chunk_decay[None, :, None, None] * state + kv_i, state

    _, prev_states = lax.scan(step, jnp.zeros((B, H, dk, dv), dtype=kv.dtype), kv)
    cross = jnp.einsum('bnchd,nbhde,hc->bnche', qc, prev_states, q_decay)
    out = (inner + cross).reshape(B, S, H, dv).astype(jnp.float32)
    mu = jnp.mean(out, axis=-1, keepdims=True)
    var = jnp.mean(jnp.square(out - mu), axis=-1, keepdims=True)
    out = ((out - mu) * lax.rsqrt(var + GN_EPS)).astype(v.dtype)
    return (out * jax.nn.silu(g)).reshape(B, S, H * dv)


def dilated_pattern(q, k, v, window, dilation):
    B, S, H, D = q.shape
    Lc = S // dilation
    wc = window // dilation
    nb = -(-Lc // ATT_BLOCK)
    Lp = nb * ATT_BLOCK

    def to_classes(t):
        return t.reshape(B, Lc, dilation, H, D).transpose(0, 2, 3, 1, 4)

    qc = jnp.pad(to_classes(q), ((0, 0), (0, 0), (0, 0), (0, Lp - Lc), (0, 0)))
    qb = qc.reshape(B, dilation, H, nb, ATT_BLOCK, D)
    kv_pad = ((0, 0), (0, 0), (0, 0), (ATT_BLOCK, Lp - Lc), (0, 0))
    kb = jnp.pad(to_classes(k), kv_pad).reshape(B, dilation, H, nb + 1, ATT_BLOCK, D)
    vb = jnp.pad(to_classes(v), kv_pad).reshape(B, dilation, H, nb + 1, ATT_BLOCK, D)
    kcat = jnp.concatenate([kb[:, :, :, :-1], kb[:, :, :, 1:]], axis=-2)
    vcat = jnp.concatenate([vb[:, :, :, :-1], vb[:, :, :, 1:]], axis=-2)
    s = jnp.einsum('brhnqd,brhnkd->brhnqk', qb, kcat).astype(jnp.float32) * (D ** -0.5)
    a = jnp.arange(ATT_BLOCK)
    kk = jnp.arange(2 * ATT_BLOCK)
    blk = jnp.arange(nb)
    dist = ATT_BLOCK + a[:, None] - kk[None, :]
    key_idx = blk[:, None] * ATT_BLOCK - ATT_BLOCK + kk[None, :]
    mask = ((dist >= 0) & (dist <= wc))[None, :, :] & (key_idx >= 0)[:, None, :]
    s = jnp.where(mask, s, -jnp.inf)
    m = jnp.max(s, axis=-1, keepdims=True)
    p = jnp.exp(s - m)
    den = jnp.sum(p, axis=-1)
    o = jnp.einsum('brhnqk,brhnkd->brhnqd', p.astype(v.dtype), vcat).astype(jnp.float32) / den[..., None]
    lse = m[..., 0] + jnp.log(den)
    o = o.reshape(B, dilation, H, Lp, D)[:, :, :, :Lc].transpose(0, 3, 1, 2, 4).reshape(B, S, H, D)
    lse = lse.reshape(B, dilation, H, Lp)[:, :, :, :Lc].transpose(0, 3, 1, 2).reshape(B, S, H)
    return o, lse


def dilated_attention_branch(q, k, v, pos):
    B, S, H, D = q.shape
    inv_freq = ROPE_THETA ** (-jnp.arange(0, ROPE_DIM, 2, dtype=jnp.float32) / ROPE_DIM)
    q = apply_rotary(q, pos, inv_freq)
    k = apply_rotary(k, pos, inv_freq)
    outs, lses = [], []
    for window, dilation in DILATED_PATTERNS:
        o, lse = dilated_pattern(q, k, v, window, dilation)
        outs.append(o)
        lses.append(lse)
    w = jax.nn.softmax(jnp.stack(lses, axis=0), axis=0)
    out = jnp.sum(w[..., None] * jnp.stack(outs, axis=0), axis=0)
    return out.astype(v.dtype).reshape(B, S, H * D)


def setup_inputs(seed: int = 0) -> dict:
    key = jax.random.key(seed)
    ks = jax.random.split(key, 12)

    def gain(k):
        return 1.0 + 0.05 * jax.random.normal(k, (DEPTH, D_MODEL), jnp.float32)

    x = jax.random.normal(ks[0], (BATCH, SEQ, D_MODEL), jnp.float32)
    offsets = jax.random.randint(ks[1], (BATCH, 1), 0, 4096, dtype=jnp.int32)
    positions = (jnp.arange(SEQ, dtype=jnp.int32)[None, :] + offsets).astype(jnp.int32)
    w_in = jax.random.normal(ks[2], (DEPTH, D_MODEL, PROJ_WIDTH), jnp.float32) * D_MODEL ** -0.5
    w_out = jax.random.normal(ks[3], (DEPTH, MIX_WIDTH, D_MODEL), jnp.float32) * MIX_WIDTH ** -0.5
    w_gate = jax.random.normal(ks[4], (DEPTH, D_MODEL, D_FF), jnp.float32) * D_MODEL ** -0.5
    w_up = jax.random.normal(ks[5], (DEPTH, D_MODEL, D_FF), jnp.float32) * D_MODEL ** -0.5
    w_down = jax.random.normal(ks[6], (DEPTH, D_FF, D_MODEL), jnp.float32) * D_FF ** -0.5
    return {'x': x, 'positions': positions, 'w_in': w_in, 'w_out': w_out,
            'g_pre_mix': gain(ks[7]), 'g_post_mix': gain(ks[8]),
            'g_pre_ffn': gain(ks[9]), 'g_post_ffn': gain(ks[10]),
            'w_gate': w_gate, 'w_up': w_up, 'w_down': w_down}


def reference(x, positions, w_in, w_out, g_pre_mix, g_post_mix, g_pre_ffn, g_post_ffn, w_gate, w_up, w_down):
    B, S, _ = x.shape
    split_idx = list(np.cumsum(PROJ_SPLITS)[:-1])
    for l in range(DEPTH):
        h = rms_norm(x, g_pre_mix[l])
        proj = h @ w_in[l]
        rq, rk, rv, rg, aq, ak, av = jnp.split(proj, split_idx, axis=-1)
        ret = retention_branch(rq.reshape(B, S, RET_HEADS, RET_QK_DIM),
                               rk.reshape(B, S, RET_HEADS, RET_QK_DIM),
                               rv.reshape(B, S, RET_HEADS, RET_V_DIM),
                               rg.reshape(B, S, RET_HEADS, RET_V_DIM), positions)
        att = dilated_attention_branch(aq.reshape(B, S, ATT_HEADS, ATT_HEAD_DIM),
                                       ak.reshape(B, S, ATT_HEADS, ATT_HEAD_DIM),
                                       av.reshape(B, S, ATT_HEADS, ATT_HEAD_DIM), positions)
        mix = jnp.concatenate([ret, att], axis=-1) @ w_out[l]
        x = x + rms_norm(mix, g_post_mix[l])
        h = rms_norm(x, g_pre_ffn[l])
        f = (jax.nn.silu(h @ w_gate[l]) * (h @ w_up[l])) @ w_down[l]
        x = x + rms_norm(f, g_post_ffn[l])
    return x
```

```python
import functools

import numpy as np
import jax
import jax.numpy as jnp
from jax import lax
from jax.experimental import pallas as pl
from jax.experimental.pallas import tpu as pltpu

F32 = jnp.float32
BF16 = jnp.bfloat16

D_MODEL = 1024
RET_HEADS = 8
RET_QK = 32
RET_V = 64
RET_QK_W = RET_HEADS * RET_QK
RET_V_W = RET_HEADS * RET_V
RET_CHUNK = 128
RET_ROT_BASE = 10000.0
ATT_HEADS = 8
ATT_DIM = 64
ATT_W = ATT_HEADS * ATT_DIM
ATT_BLOCK = 128
DILATED_PATTERNS = ((128, 1), (512, 4), (2048, 16))
ROPE_THETA = 500000.0
ROPE_DIM = ATT_DIM // 4
D_FF = 2816
RMS_EPS = 1e-6
GN_EPS = 1e-5
PROJ_WIDTH = 2 * RET_QK_W + 2 * RET_V_W + 3 * ATT_W

LANES = 128
FF_CHUNK = 256
N_FF_CHUNKS = D_FF // FF_CHUNK
VMEM_LIMIT = 56 * 1024 * 1024

TRIG_SIN_OFF = 32
TRIG_ONE_LANE = 24


def _rms(x, g):
    return x * lax.rsqrt(jnp.mean(x * x, axis=-1, keepdims=True) + RMS_EPS) * g


def _rot_tables():
    ret_f = 1.0 / (RET_ROT_BASE ** jnp.linspace(0.0, 1.0, RET_QK // 2, dtype=F32))
    att_f = ROPE_THETA ** (-jnp.arange(0, ROPE_DIM, 2, dtype=F32) / ROPE_DIM)
    half = jnp.concatenate([ret_f, att_f, jnp.zeros((8,), F32)])
    freq = jnp.concatenate([half, half, jnp.zeros((LANES - 64,), F32)])[None, :]
    e = np.zeros((LANES, 4 * LANES), np.float32)
    for c in range(LANES):
        d = c % RET_QK
        e[d % 16, c] = 1.0
        e[TRIG_SIN_OFF + d % 16, LANES + c] = -1.0 if d < 16 else 1.0
        d = c % ATT_DIM
        if d < ROPE_DIM:
            e[16 + d % 8, 2 * LANES + c] = 1.0
            e[TRIG_SIN_OFF + 16 + d % 8, 3 * LANES + c] = -1.0 if d < 8 else 1.0
        else:
            e[TRIG_ONE_LANE, 2 * LANES + c] = 1.0
    return freq, jnp.asarray(e, BF16)


def _rotate(x, cos, sin, half, group):
    lane = lax.broadcasted_iota(jnp.int32, x.shape, 1)
    up = pltpu.roll(x, LANES - half, 1)
    dn = pltpu.roll(x, half, 1)
    partner = jnp.where((lane % group) < half, up, dn)
    return x * cos + partner * sin


def _in_proj_kernel(x_ref, pos_ref, g_ref, w_ref, freq_ref, e_ref,
                    rq_ref, rk_ref, rv_ref, rg_ref, aq_ref, ak_ref, av_ref):
    x = x_ref[...]
    h = _rms(x, g_ref[...]).astype(BF16)

    ang = pos_ref[...].astype(F32) * freq_ref[...]
    lane = lax.broadcasted_iota(jnp.int32, ang.shape, 1)
    trig = jnp.where(lane < TRIG_SIN_OFF, jnp.cos(ang), jnp.sin(ang))
    hi = trig.astype(BF16)
    lo = (trig - hi.astype(F32)).astype(BF16)
    e = e_ref[...]
    tab = (jnp.dot(hi, e, preferred_element_type=F32)
           + jnp.dot(lo, e, preferred_element_type=F32))
    cos_r, sin_r = tab[:, 0:LANES], tab[:, LANES:2 * LANES]
    cos_a, sin_a = tab[:, 2 * LANES:3 * LANES], tab[:, 3 * LANES:]

    def proj(c0, width):
        return jnp.dot(h, w_ref[:, c0:c0 + width], preferred_element_type=F32)

    c = 0
    p = proj(c, RET_QK_W)
    for j in range(RET_QK_W // LANES):
        rq_ref[:, j * LANES:(j + 1) * LANES] = _rotate(
            p[:, j * LANES:(j + 1) * LANES], cos_r, sin_r, RET_QK // 2, RET_QK).astype(BF16)
    c += RET_QK_W
    p = proj(c, RET_QK_W)
    for j in range(RET_QK_W // LANES):
        rk_ref[:, j * LANES:(j + 1) * LANES] = (_rotate(
            p[:, j * LANES:(j + 1) * LANES], cos_r, sin_r, RET_QK // 2, RET_QK)
            * (RET_QK ** -0.5)).astype(BF16)
    c += RET_QK_W
    rv_ref[...] = proj(c, RET_V_W).astype(BF16)
    c += RET_V_W
    rg_ref[...] = proj(c, RET_V_W).astype(BF16)
    c += RET_V_W
    p = proj(c, ATT_W)
    for j in range(ATT_W // LANES):
        aq_ref[:, j * LANES:(j + 1) * LANES] = _rotate(
            p[:, j * LANES:(j + 1) * LANES], cos_a, sin_a, ROPE_DIM // 2, ATT_DIM) * (ATT_DIM ** -0.5)
    c += ATT_W
    p = proj(c, ATT_W)
    for j in range(ATT_W // LANES):
        ak_ref[:, j * LANES:(j + 1) * LANES] = _rotate(
            p[:, j * LANES:(j + 1) * LANES], cos_a, sin_a, ROPE_DIM // 2, ATT_DIM)
    c += ATT_W
    av_ref[...] = proj(c, ATT_W)


def _const_spec(shape):
    return pl.BlockSpec(shape, lambda *_: (0,) * len(shape), pipeline_mode=pl.Buffered(1))


def _in_proj(x2, pos2, g, w_bf, tm):
    t = x2.shape[0]
    freq, e = _rot_tables()
    row = lambda w: pl.BlockSpec((tm, w), lambda i: (i, 0))
    out_shape = (
        jax.ShapeDtypeStruct((t, RET_QK_W), BF16), jax.ShapeDtypeStruct((t, RET_QK_W), BF16),
        jax.ShapeDtypeStruct((t, RET_V_W), BF16), jax.ShapeDtypeStruct((t, RET_V_W), BF16),
        jax.ShapeDtypeStruct((t, ATT_W), F32), jax.ShapeDtypeStruct((t, ATT_W), F32),
        jax.ShapeDtypeStruct((t, ATT_W), F32))
    return pl.pallas_call(
        _in_proj_kernel,
        out_shape=out_shape,
        grid=(t // tm,),
        in_specs=[row(D_MODEL), row(1), _const_spec((1, D_MODEL)), _const_spec((D_MODEL, PROJ_WIDTH)),
                  _const_spec((1, LANES)), _const_spec((LANES, 4 * LANES))],
        out_specs=(row(RET_QK_W), row(RET_QK_W), row(RET_V_W), row(RET_V_W),
                   row(ATT_W), row(ATT_W), row(ATT_W)),
        compiler_params=pltpu.CompilerParams(dimension_semantics=("arbitrary",),
                                             vmem_limit_bytes=VMEM_LIMIT),
        name="in_proj",
    )(x2, pos2, g, w_bf, freq, e)


def _retention_consts():
    h, c = RET_HEADS, RET_CHUNK
    log_g = jnp.log1p(-jnp.exp2(-5.0 - jnp.arange(h, dtype=F32)))
    idx = jnp.arange(c, dtype=F32)
    diff = idx[:, None] - idx[None, :]
    inner = jnp.where(diff >= 0, jnp.exp(log_g[:, None, None] * jnp.maximum(diff, 0.0)), 0.0)
    q_decay = jnp.exp(log_g[:, None] * (idx + 1.0)[None, :])
    k_decay = jnp.exp(log_g[:, None] * (c - 1.0 - idx)[None, :])
    chunk_decay = jnp.exp(log_g * c)
    dec = jnp.concatenate([inner[0::2], inner[1::2]], axis=-1)
    qd = jnp.repeat(q_decay.T, RET_V, axis=1)
    kd = jnp.repeat(k_decay.T, RET_QK, axis=1)
    cd = jnp.repeat(chunk_decay, RET_V)[None, :]
    rows = np.arange(RET_QK_W)[:, None] // RET_QK
    cols = np.arange(RET_V_W)[None, :] // RET_V
    bd = jnp.asarray(rows == cols, F32)
    gi = np.arange(2 * LANES) // RET_V
    gn = jnp.asarray(gi[:, None] == gi[None, :], BF16)
    return dec, qd, kd, cd, bd, gn


def _retention_kernel(q_ref, k_ref, v_ref, g_ref, dec_ref, qd_ref, kd_ref, cd_ref, bd_ref, gn_ref,
                      o_ref, state_ref):
    n_chunks = q_ref.shape[0] // RET_CHUNK
    state_ref[...] = jnp.zeros_like(state_ref)
    lane_qk = lax.broadcasted_iota(jnp.int32, (RET_CHUNK, RET_QK_W), 1)
    lane_v = lax.broadcasted_iota(jnp.int32, (RET_CHUNK, LANES), 1)
    nt = (((1,), (1,)), ((), ()))
    tn = (((0,), (0,)), ((), ()))

    def chunk(n, carry):
        r0 = pl.multiple_of(n * RET_CHUNK, RET_CHUNK)
        q = q_ref[pl.ds(r0, RET_CHUNK), :]
        k = k_ref[pl.ds(r0, RET_CHUNK), :]
        v = v_ref[pl.ds(r0, RET_CHUNK), :]
        zero_k = jnp.zeros_like(k)
        inner = []
        for j in range(RET_HEADS // 2):
            lo = 2 * j * RET_QK
            ka = jnp.where((lane_qk >= lo) & (lane_qk < lo + RET_QK), k, zero_k)
            kb = jnp.where((lane_qk >= lo + RET_QK) & (lane_qk < lo + 2 * RET_QK), k, zero_k)
            s = lax.dot_general(q, jnp.concatenate([ka, kb], axis=0), nt,
                                preferred_element_type=F32)
            p = (s * dec_ref[j]).astype(BF16)
            vj = v[:, j * LANES:(j + 1) * LANES]
            zero_v = jnp.zeros_like(vj)
            vv = jnp.concatenate([jnp.where(lane_v < RET_V, vj, zero_v),
                                  jnp.where(lane_v >= RET_V, vj, zero_v)], axis=0)
            inner.append(jnp.dot(p, vv, preferred_element_type=F32))
        state = state_ref[...]
        cross = jnp.dot(q, state.astype(BF16), preferred_element_type=F32) * qd_ref[...]
        out = jnp.concatenate(inner, axis=1) + cross

        kdk = (k.astype(F32) * kd_ref[...]).astype(BF16)
        kv = lax.dot_general(kdk, v, tn, preferred_element_type=F32)
        state_ref[...] = state * cd_ref[...] + kv * bd_ref[...]

        gn = gn_ref[...]
        halves = []
        for t in range(2):
            o_t = out[:, t * 2 * LANES:(t + 1) * 2 * LANES]
            mu = jnp.dot(o_t.astype(BF16), gn, preferred_element_type=F32) * (1.0 / RET_V)
            d = o_t - mu
            var = jnp.dot((d * d).astype(BF16), gn, preferred_element_type=F32) * (1.0 / RET_V)
            halves.append(d * lax.rsqrt(var + GN_EPS))
        y = jnp.concatenate(halves, axis=1)
        g = g_ref[pl.ds(r0, RET_CHUNK), :].astype(F32)
        o_ref[pl.ds(r0, RET_CHUNK), :] = (y * (g * jax.nn.sigmoid(g))).astype(BF16)
        return carry

    lax.fori_loop(0, n_chunks, chunk, 0)


def _retention(rq, rk, rv, rg, batch, seq):
    consts = _retention_consts()
    seq_spec = lambda w: pl.BlockSpec((seq, w), lambda b: (b, 0))
    return pl.pallas_call(
        _retention_kernel,
        out_shape=jax.ShapeDtypeStruct((batch * seq, RET_V_W), BF16),
        grid=(batch,),
        in_specs=[seq_spec(RET_QK_W), seq_spec(RET_QK_W), seq_spec(RET_V_W), seq_spec(RET_V_W)]
                 + [_const_spec(c.shape) for c in consts],
        out_specs=seq_spec(RET_V_W),
        scratch_shapes=[pltpu.VMEM((RET_QK_W, RET_V_W), F32)],
        compiler_params=pltpu.CompilerParams(dimension_semantics=("arbitrary",),
                                             vmem_limit_bytes=VMEM_LIMIT),
        name="retention",
    )(rq, rk, rv, rg, *consts)


def _attend(qb, kb, vb, first):
    nk = kb.shape[0]
    a = lax.broadcasted_iota(jnp.int32, (ATT_BLOCK, nk), 0)
    kk = lax.broadcasted_iota(jnp.int32, (ATT_BLOCK, nk), 1)
    if first:
        mask = kk <= a
    else:
        dist = ATT_BLOCK + a - kk
        mask = (dist >= 0) & (dist <= ATT_BLOCK)
    lane = lax.broadcasted_iota(jnp.int32, (ATT_BLOCK, LANES), 1)
    kbf = kb.astype(BF16)
    vext = jnp.concatenate([vb.astype(BF16), jnp.ones((nk, LANES), BF16)], axis=1)
    outs, lses = [], []
    for head_lanes in (lane < ATT_DIM, lane >= ATT_DIM):
        qm = jnp.where(head_lanes, qb, 0.0).astype(BF16)
        s = lax.dot_general(qm, kbf, (((1,), (1,)), ((), ())), preferred_element_type=F32)
        s = jnp.where(mask, s, -jnp.inf)
        m = jnp.max(s, axis=-1, keepdims=True)
        p = jnp.exp(s - m)
        pv = jnp.dot(p.astype(BF16), vext, preferred_element_type=F32)
        den = pv[:, LANES:]
        outs.append(pv[:, :LANES] / den)
        lses.append(m + jnp.log(den))
    first_head = lane < ATT_DIM
    return jnp.where(first_head, outs[0], outs[1]), jnp.where(first_head, lses[0], lses[1])


def _dilated_kernel(q_ref, k_ref, v_ref, o_ref, *scratch):
    seq = q_ref.shape[0]
    blk = ATT_BLOCK
    for (window, dil), o_s, l_s in zip(DILATED_PATTERNS, scratch[0::2], scratch[1::2]):
        n_blocks = seq // dil // blk
        stride = dil if dil > 1 else None

        def rows(ref, start, size):
            return ref[pl.ds(start, size, stride=stride), :]

        def put(ref, start, val):
            ref[pl.ds(start, blk, stride=stride), :] = val

        def one_class(r, carry):
            o, l = _attend(rows(q_ref, r, blk), rows(k_ref, r, blk), rows(v_ref, r, blk), True)
            put(o_s, r, o)
            put(l_s, r, l)

            def later(b, c):
                q0 = r + b * (blk * dil)
                k0 = q0 - blk * dil
                if dil == 1:
                    q0 = pl.multiple_of(q0, blk)
                    k0 = pl.multiple_of(k0, blk)
                o, l = _attend(rows(q_ref, q0, blk), rows(k_ref, k0, 2 * blk),
                               rows(v_ref, k0, 2 * blk), False)
                put(o_s, q0, o)
                put(l_s, q0, l)
                return c

            if n_blocks > 1:
                lax.fori_loop(1, n_blocks, later, 0)
            return carry

        if dil == 1:
            one_class(0, 0)
        else:
            lax.fori_loop(0, dil, one_class, 0)

    def combine(i, carry):
        r0 = pl.multiple_of(i * blk, blk)
        sl = pl.ds(r0, blk)
        ls = [l_s[sl, :] for l_s in scratch[1::2]]
        m = functools.reduce(jnp.maximum, ls)
        ws = [jnp.exp(l - m) for l in ls]
        num = functools.reduce(lambda x, y: x + y, [w * o_s[sl, :] for w, o_s in zip(ws, scratch[0::2])])
        den = functools.reduce(lambda x, y: x + y, ws)
        o_ref[sl, :] = (num / den).astype(o_ref.dtype)
        return carry

    lax.fori_loop(0, seq // blk, combine, 0)


def _dilated(aq, ak, av, batch, seq):
    for window, dil in DILATED_PATTERNS:
        assert window // dil == ATT_BLOCK and (seq // dil) % ATT_BLOCK == 0
    spec = pl.BlockSpec((seq, LANES), lambda b, hp: (b, hp))
    return pl.pallas_call(
        _dilated_kernel,
        out_shape=jax.ShapeDtypeStruct((batch * seq, ATT_W), BF16),
        grid=(batch, ATT_W // LANES),
        in_specs=[spec, spec, spec],
        out_specs=spec,
        scratch_shapes=[pltpu.VMEM((seq, LANES), F32) for _ in range(2 * len(DILATED_PATTERNS))],
        compiler_params=pltpu.CompilerParams(dimension_semantics=("arbitrary", "arbitrary"),
                                             vmem_limit_bytes=VMEM_LIMIT),
        name="dilated",
    )(aq, ak, av)


def _out_ffn_kernel(ret_ref, att_ref, x_ref, wor_ref, woa_ref, g1_ref, g2_ref, g3_ref, wgu_ref, wd_ref,
                    o_ref, acc_ref):
    mix = (jnp.dot(ret_ref[...], wor_ref[...], preferred_element_type=F32)
           + jnp.dot(att_ref[...], woa_ref[...], preferred_element_type=F32))
    x1 = x_ref[...] + _rms(mix, g1_ref[...])
    h = _rms(x1, g2_ref[...]).astype(BF16)
    acc_ref[...] = jnp.zeros_like(acc_ref)

    def ff(c, carry):
        gu = jnp.dot(h, wgu_ref[c], preferred_element_type=F32)
        a, u = gu[:, :FF_CHUNK], gu[:, FF_CHUNK:]
        act = (a * jax.nn.sigmoid(a) * u).astype(BF16)
        acc_ref[...] += jnp.dot(act, wd_ref[c], preferred_element_type=F32)
        return carry

    lax.fori_loop(0, N_FF_CHUNKS, ff, 0)
    o_ref[...] = x1 + _rms(acc_ref[...], g3_ref[...])


def _out_ffn(ret, att, x2, wo_r, wo_a, g1, g2, g3, wgu, wd, tm):
    t = x2.shape[0]
    row = lambda w: pl.BlockSpec((tm, w), lambda i: (i, 0))
    return pl.pallas_call(
        _out_ffn_kernel,
        out_shape=jax.ShapeDtypeStruct((t, D_MODEL), F32),
        grid=(t // tm,),
        in_specs=[row(RET_V_W), row(ATT_W), row(D_MODEL),
                  _const_spec(wo_r.shape), _const_spec(wo_a.shape),
                  _const_spec(g1.shape), _const_spec(g2.shape), _const_spec(g3.shape),
                  _const_spec(wgu.shape), _const_spec(wd.shape)],
        out_specs=row(D_MODEL),
        scratch_shapes=[pltpu.VMEM((tm, D_MODEL), F32)],
        compiler_params=pltpu.CompilerParams(dimension_semantics=("arbitrary",),
                                             vmem_limit_bytes=VMEM_LIMIT),
        name="out_ffn",
    )(ret, att, x2, wo_r, wo_a, g1, g2, g3, wgu, wd)


def _layer(x2, pos2, batch, seq, w_in, w_out, g_pre_mix, g_post_mix, g_pre_ffn, g_post_ffn,
           w_gate, w_up, w_down, tm):
    row = lambda g: g.reshape(1, D_MODEL).astype(F32)
    rq, rk, rv, rg, aq, ak, av = _in_proj(x2, pos2, row(g_pre_mix), w_in.astype(BF16), tm)
    ret = _retention(rq, rk, rv, rg, batch, seq)
    att = _dilated(aq, ak, av, batch, seq)
    w_out_bf = w_out.astype(BF16)
    wg = w_gate.astype(BF16).reshape(D_MODEL, N_FF_CHUNKS, FF_CHUNK)
    wu = w_up.astype(BF16).reshape(D_MODEL, N_FF_CHUNKS, FF_CHUNK)
    wgu = jnp.concatenate([wg, wu], axis=-1).transpose(1, 0, 2)
    wd = w_down.astype(BF16).reshape(N_FF_CHUNKS, FF_CHUNK, D_MODEL)
    return _out_ffn(ret, att, x2, w_out_bf[:RET_V_W], w_out_bf[RET_V_W:], row(g_post_mix),
                    row(g_pre_ffn), row(g_post_ffn), wgu, wd, tm)


def kernel(x, positions, w_in, w_out, g_pre_mix, g_post_mix, g_pre_ffn, g_post_ffn, w_gate, w_up, w_down):
    batch, seq, d = x.shape
    assert d == D_MODEL and seq % (ATT_BLOCK * DILATED_PATTERNS[-1][1]) == 0
    tm = 512
    x2 = x.reshape(batch * seq, d)
    pos2 = positions.reshape(batch * seq, 1)
    for l in range(w_in.shape[0]):
        x2 = _layer(x2, pos2, batch, seq, w_in[l], w_out[l], g_pre_mix[l], g_post_mix[l], g_pre_ffn[l],
                    g_post_ffn[l], w_gate[l], w_up[l], w_down[l], tm)
    return x2.reshape(batch, seq, d)
```

```python
import functools

import numpy as np
import jax
import jax.numpy as jnp
from jax import lax
from jax.experimental import pallas as pl
from jax.experimental.pallas import tpu as pltpu

F32 = jnp.float32
BF16 = jnp.bfloat16

D_MODEL = 1024
RET_HEADS = 8
RET_QK = 32
RET_V = 64
RET_QK_W = RET_HEADS * RET_QK
RET_V_W = RET_HEADS * RET_V
RET_CHUNK = 128
RET_ROT_BASE = 10000.0
ATT_HEADS = 8
ATT_DIM = 64
ATT_W = ATT_HEADS * ATT_DIM
ATT_BLOCK = 128
DILATED_PATTERNS = ((128, 1), (512, 4), (2048, 16))
ROPE_THETA = 500000.0
ROPE_DIM = ATT_DIM // 4
D_FF = 2816
RMS_EPS = 1e-6
GN_EPS = 1e-5
PROJ_WIDTH = 2 * RET_QK_W + 2 * RET_V_W + 3 * ATT_W

LANES = 128
FF_CHUNK = 256
N_FF_CHUNKS = D_FF // FF_CHUNK
VMEM_LIMIT = 56 * 1024 * 1024
LOG2E = 1.4426950408889634
STAGE_DIL = 4

TRIG_SIN_OFF = 32
TRIG_ONE_LANE = 24


def _rms(x, g):
    return x * lax.rsqrt(jnp.mean(x * x, axis=-1, keepdims=True) + RMS_EPS) * g


def _rot_tables():
    ret_f = 1.0 / (RET_ROT_BASE ** jnp.linspace(0.0, 1.0, RET_QK // 2, dtype=F32))
    att_f = ROPE_THETA ** (-jnp.arange(0, ROPE_DIM, 2, dtype=F32) / ROPE_DIM)
    half = jnp.concatenate([ret_f, att_f, jnp.zeros((8,), F32)])
    freq = jnp.concatenate([half, half, jnp.zeros((LANES - 64,), F32)])[None, :]
    e = np.zeros((LANES, 4 * LANES), np.float32)
    for c in range(LANES):
        d = c % RET_QK
        e[d % 16, c] = 1.0
        e[TRIG_SIN_OFF + d % 16, LANES + c] = -1.0 if d < 16 else 1.0
        d = c % ATT_DIM
        if d < ROPE_DIM:
            e[16 + d % 8, 2 * LANES + c] = 1.0
            e[TRIG_SIN_OFF + 16 + d % 8, 3 * LANES + c] = -1.0 if d < 8 else 1.0
        else:
            e[TRIG_ONE_LANE, 2 * LANES + c] = 1.0
    return freq, jnp.asarray(e, BF16)


def _rotate(x, cos, sin, half, group):
    lane = lax.broadcasted_iota(jnp.int32, x.shape, 1)
    up = pltpu.roll(x, LANES - half, 1)
    dn = pltpu.roll(x, half, 1)
    partner = jnp.where((lane % group) < half, up, dn)
    return x * cos + partner * sin


def _in_proj_kernel(x_ref, pos_ref, g_ref, w_ref, freq_ref, e_ref,
                    rq_ref, rk_ref, rv_ref, rg_ref, aq_ref, ak_ref, av_ref):
    x = x_ref[...]
    h = _rms(x, g_ref[...]).astype(BF16)

    ang = pos_ref[...].astype(F32) * freq_ref[...]
    lane = lax.broadcasted_iota(jnp.int32, ang.shape, 1)
    trig = jnp.where(lane < TRIG_SIN_OFF, jnp.cos(ang), jnp.sin(ang))
    hi = trig.astype(BF16)
    lo = (trig - hi.astype(F32)).astype(BF16)
    e = e_ref[...]
    tab = (jnp.dot(hi, e, preferred_element_type=F32)
           + jnp.dot(lo, e, preferred_element_type=F32))
    cos_r, sin_r = tab[:, 0:LANES], tab[:, LANES:2 * LANES]
    cos_a, sin_a = tab[:, 2 * LANES:3 * LANES], tab[:, 3 * LANES:]

    def proj(c0, width):
        return jnp.dot(h, w_ref[:, c0:c0 + width], preferred_element_type=F32)

    c = 0
    p = proj(c, RET_QK_W)
    for j in range(RET_QK_W // LANES):
        rq_ref[:, j * LANES:(j + 1) * LANES] = _rotate(
            p[:, j * LANES:(j + 1) * LANES], cos_r, sin_r, RET_QK // 2, RET_QK).astype(BF16)
    c += RET_QK_W
    p = proj(c, RET_QK_W)
    for j in range(RET_QK_W // LANES):
        rk_ref[:, j * LANES:(j + 1) * LANES] = (_rotate(
            p[:, j * LANES:(j + 1) * LANES], cos_r, sin_r, RET_QK // 2, RET_QK)
            * (RET_QK ** -0.5)).astype(BF16)
    c += RET_QK_W
    rv_ref[...] = proj(c, RET_V_W).astype(BF16)
    c += RET_V_W
    rg_ref[...] = proj(c, RET_V_W).astype(BF16)
    c += RET_V_W
    p = proj(c, ATT_W)
    for j in range(ATT_W // LANES):
        aq_ref[:, j * LANES:(j + 1) * LANES] = _rotate(
            p[:, j * LANES:(j + 1) * LANES], cos_a, sin_a, ROPE_DIM // 2, ATT_DIM) * (ATT_DIM ** -0.5 * LOG2E)
    c += ATT_W
    p = proj(c, ATT_W)
    for j in range(ATT_W // LANES):
        ak_ref[:, j * LANES:(j + 1) * LANES] = _rotate(
            p[:, j * LANES:(j + 1) * LANES], cos_a, sin_a, ROPE_DIM // 2, ATT_DIM)
    c += ATT_W
    av_ref[...] = proj(c, ATT_W)


def _const_spec(shape):
    return pl.BlockSpec(shape, lambda *_: (0,) * len(shape), pipeline_mode=pl.Buffered(1))


def _in_proj(x2, pos2, g, w_bf, tm):
    t = x2.shape[0]
    freq, e = _rot_tables()
    row = lambda w: pl.BlockSpec((tm, w), lambda i: (i, 0))
    out_shape = (
        jax.ShapeDtypeStruct((t, RET_QK_W), BF16), jax.ShapeDtypeStruct((t, RET_QK_W), BF16),
        jax.ShapeDtypeStruct((t, RET_V_W), BF16), jax.ShapeDtypeStruct((t, RET_V_W), BF16),
        jax.ShapeDtypeStruct((t, ATT_W), F32), jax.ShapeDtypeStruct((t, ATT_W), F32),
        jax.ShapeDtypeStruct((t, ATT_W), F32))
    return pl.pallas_call(
        _in_proj_kernel,
        out_shape=out_shape,
        grid=(t // tm,),
        in_specs=[row(D_MODEL), row(1), _const_spec((1, D_MODEL)), _const_spec((D_MODEL, PROJ_WIDTH)),
                  _const_spec((1, LANES)), _const_spec((LANES, 4 * LANES))],
        out_specs=(row(RET_QK_W), row(RET_QK_W), row(RET_V_W), row(RET_V_W),
                   row(ATT_W), row(ATT_W), row(ATT_W)),
        compiler_params=pltpu.CompilerParams(dimension_semantics=("arbitrary",),
                                             vmem_limit_bytes=VMEM_LIMIT),
        name="in_proj",
    )(x2, pos2, g, w_bf, freq, e)


def _retention_consts():
    h, c = RET_HEADS, RET_CHUNK
    log_g = jnp.log1p(-jnp.exp2(-5.0 - jnp.arange(h, dtype=F32)))
    idx = jnp.arange(c, dtype=F32)
    diff = idx[:, None] - idx[None, :]
    inner = jnp.where(diff >= 0, jnp.exp(log_g[:, None, None] * jnp.maximum(diff, 0.0)), 0.0)
    q_decay = jnp.exp(log_g[:, None] * (idx + 1.0)[None, :])
    k_decay = jnp.exp(log_g[:, None] * (c - 1.0 - idx)[None, :])
    chunk_decay = jnp.exp(log_g * c)
    dec = jnp.concatenate([inner[0::2], inner[1::2]], axis=-1)
    qd = jnp.repeat(q_decay.T, RET_V, axis=1)
    kd = jnp.repeat(k_decay.T, RET_QK, axis=1)
    cd = jnp.repeat(chunk_decay, RET_V)[None, :]
    rows = np.arange(RET_QK_W)[:, None] // RET_QK
    cols = np.arange(RET_V_W)[None, :] // RET_V
    bd = jnp.asarray(rows == cols, F32)
    gi = np.arange(2 * LANES) // RET_V
    gn = jnp.asarray(gi[:, None] == gi[None, :], BF16)
    return dec, qd, kd, cd, bd, gn


def _retention_kernel(q_ref, k_ref, v_ref, g_ref, dec_ref, qd_ref, kd_ref, cd_ref, bd_ref, gn_ref,
                      o_ref, state_ref):
    n_chunks = q_ref.shape[0] // RET_CHUNK
    state_ref[...] = jnp.zeros_like(state_ref)
    lane_qk = lax.broadcasted_iota(jnp.int32, (RET_CHUNK, RET_QK_W), 1)
    lane_v = lax.broadcasted_iota(jnp.int32, (RET_CHUNK, LANES), 1)
    nt = (((1,), (1,)), ((), ()))
    tn = (((0,), (0,)), ((), ()))

    def chunk(n, carry):
        r0 = pl.multiple_of(n * RET_CHUNK, RET_CHUNK)
        q = q_ref[pl.ds(r0, RET_CHUNK), :]
        k = k_ref[pl.ds(r0, RET_CHUNK), :]
        v = v_ref[pl.ds(r0, RET_CHUNK), :]
        zero_k = jnp.zeros_like(k)
        inner = []
        for j in range(RET_HEADS // 2):
            lo = 2 * j * RET_QK
            ka = jnp.where((lane_qk >= lo) & (lane_qk < lo + RET_QK), k, zero_k)
            kb = jnp.where((lane_qk >= lo + RET_QK) & (lane_qk < lo + 2 * RET_QK), k, zero_k)
            s = lax.dot_general(q, jnp.concatenate([ka, kb], axis=0), nt,
                                preferred_element_type=F32)
            p = (s * dec_ref[j]).astype(BF16)
            vj = v[:, j * LANES:(j + 1) * LANES]
            zero_v = jnp.zeros_like(vj)
            vv = jnp.concatenate([jnp.where(lane_v < RET_V, vj, zero_v),
                                  jnp.where(lane_v >= RET_V, vj, zero_v)], axis=0)
            inner.append(jnp.dot(p, vv, preferred_element_type=F32))
        state = state_ref[...]
        cross = jnp.dot(q, state.astype(BF16), preferred_element_type=F32) * qd_ref[...]
        out = jnp.concatenate(inner, axis=1) + cross

        kdk = (k.astype(F32) * kd_ref[...]).astype(BF16)
        kv = lax.dot_general(kdk, v, tn, preferred_element_type=F32)
        state_ref[...] = state * cd_ref[...] + kv * bd_ref[...]

        gn = gn_ref[...]
        halves = []
        for t in range(2):
            o_t = out[:, t * 2 * LANES:(t + 1) * 2 * LANES]
            mu = jnp.dot(o_t.astype(BF16), gn, preferred_element_type=F32) * (1.0 / RET_V)
            d = o_t - mu
            var = jnp.dot((d * d).astype(BF16), gn, preferred_element_type=F32) * (1.0 / RET_V)
            halves.append(d * lax.rsqrt(var + GN_EPS))
        y = jnp.concatenate(halves, axis=1)
        g = g_ref[pl.ds(r0, RET_CHUNK), :].astype(F32)
        o_ref[pl.ds(r0, RET_CHUNK), :] = (y * (g * jax.nn.sigmoid(g))).astype(BF16)
        return carry

    lax.fori_loop(0, n_chunks, chunk, 0)


def _retention(rq, rk, rv, rg, batch, seq):
    consts = _retention_consts()
    seq_spec = lambda w: pl.BlockSpec((seq, w), lambda b: (b, 0))
    return pl.pallas_call(
        _retention_kernel,
        out_shape=jax.ShapeDtypeStruct((batch * seq, RET_V_W), BF16),
        grid=(batch,),
        in_specs=[seq_spec(RET_QK_W), seq_spec(RET_QK_W), seq_spec(RET_V_W), seq_spec(RET_V_W)]
                 + [_const_spec(c.shape) for c in consts],
        out_specs=seq_spec(RET_V_W),
        scratch_shapes=[pltpu.VMEM((RET_QK_W, RET_V_W), F32)],
        compiler_params=pltpu.CompilerParams(dimension_semantics=("arbitrary",),
                                             vmem_limit_bytes=VMEM_LIMIT),
        name="retention",
    )(rq, rk, rv, rg, *consts)


def _attend(qb, kb, vb, bias):
    nk = kb.shape[0]
    lane = lax.broadcasted_iota(jnp.int32, (ATT_BLOCK, LANES), 1)
    first_head = lane < ATT_DIM
    kbf = kb.astype(BF16)
    vext = jnp.concatenate([vb.astype(BF16), jnp.ones((nk, LANES), BF16)], axis=1)
    res = []
    for head_lanes in (first_head, jnp.logical_not(first_head)):
        qm = jnp.where(head_lanes, qb, 0.0).astype(BF16)
        s = lax.dot_general(qm, kbf, (((1,), (1,)), ((), ())), preferred_element_type=F32) + bias
        m = jnp.max(s, axis=-1, keepdims=True)
        p = jnp.exp2(s - m)
        pv = jnp.dot(p.astype(BF16), vext, preferred_element_type=F32)
        res.append((pv[:, :LANES], pv[:, LANES:], m))
    return tuple(jnp.where(first_head, a, b) for a, b in zip(*res))


def _attend_first(qb, kb, vb, bias2):
    lane = lax.broadcasted_iota(jnp.int32, (ATT_BLOCK, LANES), 1)
    first_head = lane < ATT_DIM
    sel_a = lambda x: jnp.where(first_head, x, 0.0).astype(BF16)
    sel_b = lambda x: jnp.where(first_head, 0.0, x).astype(BF16)
    one = jnp.ones((ATT_BLOCK, LANES), F32)
    kcat = jnp.concatenate([sel_a(kb), sel_b(kb)], axis=0)
    s = lax.dot_general(qb.astype(BF16), kcat, (((1,), (1,)), ((), ())), preferred_element_type=F32) + bias2
    ma = jnp.max(s[:, :ATT_BLOCK], axis=-1, keepdims=True)
    mb = jnp.max(s[:, ATT_BLOCK:], axis=-1, keepdims=True)
    p = jnp.concatenate([jnp.exp2(s[:, :ATT_BLOCK] - ma), jnp.exp2(s[:, ATT_BLOCK:] - mb)], axis=1)
    vcat = jnp.concatenate([jnp.concatenate([sel_a(vb), sel_a(one)], axis=1),
                            jnp.concatenate([sel_b(vb), sel_b(one)], axis=1)], axis=0)
    pv = jnp.dot(p.astype(BF16), vcat, preferred_element_type=F32)
    return pv[:, :LANES], pv[:, LANES:], jnp.where(first_head, ma, mb)


def _dilated_kernel(q_ref, k_ref, v_ref, o_ref, bias_ref, bias2_ref, q4_ref, k4_ref, v4_ref, *scratch):
    seq = q_ref.shape[0]
    blk = ATT_BLOCK
    sd = STAGE_DIL
    lc = seq // sd
    a = lax.broadcasted_iota(jnp.int32, (blk, 2 * blk), 0)
    kk = lax.broadcasted_iota(jnp.int32, (blk, 2 * blk), 1)
    dist = blk + a - kk
    bias_ref[...] = jnp.where((dist >= 0) & (dist <= blk), 0.0, -jnp.inf)
    causal = jnp.where(kk % blk <= a, 0.0, -jnp.inf)
    bias2_ref[...] = causal
    for r in range(sd):
        for src, dst in ((q_ref, q4_ref), (k_ref, k4_ref), (v_ref, v4_ref)):
            dst[r * lc:(r + 1) * lc, :] = src[pl.ds(r, lc, stride=sd), :]

    def view(p_idx, cls, b, n):
        dil = DILATED_PATTERNS[p_idx][1]
        if dil == 1:
            return (q_ref, k_ref, v_ref), pl.ds(b * blk, n)
        if dil == sd:
            return (q4_ref, k4_ref, v4_ref), pl.ds(cls * lc + b * blk, n)
        return (q4_ref, k4_ref, v4_ref), pl.ds((cls % sd) * lc + cls // sd + b * blk * sd, n, stride=sd)

    def run(tasks, first):
        loaded = []
        for p_idx, cls, b in tasks:
            (qr, kr, vr), qs = view(p_idx, cls, b, blk)
            _, ks = (None, qs) if first else view(p_idx, cls, b - 1, 2 * blk)
            loaded.append((qr[qs, :], kr[ks, :], vr[ks, :], qs))
        if first:
            bias2 = bias2_ref[...]
            results = [_attend_first(qb, kb, vb, bias2) for qb, kb, vb, _ in loaded]
        else:
            bias = bias_ref[...]
            results = [_attend(qb, kb, vb, bias) for qb, kb, vb, _ in loaded]
        for (p_idx, _, _), (_, _, _, dst), res in zip(tasks, loaded, results):
            for ref, val in zip(scratch[3 * p_idx:3 * p_idx + 3], res):
                ref[dst, :] = val

    n_blocks = [seq // dil // blk for _, dil in DILATED_PATTERNS]
    run([(p, cls, 0) for p, (_, dil) in enumerate(DILATED_PATTERNS) for cls in range(dil)], True)
    for p, (_, dil) in enumerate(DILATED_PATTERNS):
        if n_blocks[p] > 1:
            run([(p, cls, b) for b in range(1, n_blocks[p]) for cls in range(dil)], False)

    def combine(c, carry):
        for r in range(sd):
            nat = pl.ds(r + c * blk * sd, blk, stride=sd)
            cls = pl.ds(pl.multiple_of(r * lc + c * blk, blk), blk)
            sls = [nat if dil == 1 else cls for _, dil in DILATED_PATTERNS]
            ms = [scratch[3 * p + 2][sl, :] for p, sl in enumerate(sls)]
            m = functools.reduce(jnp.maximum, ms)
            ws = [jnp.exp2(mp - m) for mp in ms]
            num = functools.reduce(lambda x, y: x + y,
                                   [w * scratch[3 * p][sl, :] for p, (w, sl) in enumerate(zip(ws, sls))])
            den = functools.reduce(lambda x, y: x + y,
                                   [w * scratch[3 * p + 1][sl, :] for p, (w, sl) in enumerate(zip(ws, sls))])
            o_ref[nat, :] = num / den
        return carry

    lax.fori_loop(0, lc // blk, combine, 0)


def _dilated(aq, ak, av, batch, seq):
    dils = [dil for _, dil in DILATED_PATTERNS]
    assert dils == [1, STAGE_DIL, STAGE_DIL * STAGE_DIL]
    for window, dil in DILATED_PATTERNS:
        assert window // dil == ATT_BLOCK and (seq // dil) % ATT_BLOCK == 0
    spec = pl.BlockSpec((seq, LANES), lambda b, hp: (b, hp))
    return pl.pallas_call(
        _dilated_kernel,
        out_shape=jax.ShapeDtypeStruct((batch * seq, ATT_W), F32),
        grid=(batch, ATT_W // LANES),
        in_specs=[spec, spec, spec],
        out_specs=spec,
        scratch_shapes=[pltpu.VMEM((ATT_BLOCK, 2 * ATT_BLOCK), F32) for _ in range(2)]
                       + [pltpu.VMEM((seq, LANES), F32) for _ in range(3 + 3 * len(DILATED_PATTERNS))],
        compiler_params=pltpu.CompilerParams(dimension_semantics=("arbitrary", "arbitrary"),
                                             vmem_limit_bytes=VMEM_LIMIT),
        name="dilated",
    )(aq, ak, av)


def _out_ffn_kernel(ret_ref, att_ref, x_ref, wor_ref, woa_ref, g1_ref, g2_ref, g3_ref, wgu_ref, wd_ref,
                    o_ref, acc_ref):
    mix = (jnp.dot(ret_ref[...], wor_ref[...], preferred_element_type=F32)
           + jnp.dot(att_ref[...].astype(BF16), woa_ref[...], preferred_element_type=F32))
    x1 = x_ref[...] + _rms(mix, g1_ref[...])
    h = _rms(x1, g2_ref[...]).astype(BF16)
    acc_ref[...] = jnp.zeros_like(acc_ref)

    def ff(c, carry):
        gu = jnp.dot(h, wgu_ref[c], preferred_element_type=F32)
        a, u = gu[:, :FF_CHUNK], gu[:, FF_CHUNK:]
        act = (a * jax.nn.sigmoid(a) * u).astype(BF16)
        acc_ref[...] += jnp.dot(act, wd_ref[c], preferred_element_type=F32)
        return carry

    lax.fori_loop(0, N_FF_CHUNKS, ff, 0)
    o_ref[...] = x1 + _rms(acc_ref[...], g3_ref[...])


def _out_ffn(ret, att, x2, wo_r, wo_a, g1, g2, g3, wgu, wd, tm):
    t = x2.shape[0]
    row = lambda w: pl.BlockSpec((tm, w), lambda i: (i, 0))
    return pl.pallas_call(
        _out_ffn_kernel,
        out_shape=jax.ShapeDtypeStruct((t, D_MODEL), F32),
        grid=(t // tm,),
        in_specs=[row(RET_V_W), row(ATT_W), row(D_MODEL),
                  _const_spec(wo_r.shape), _const_spec(wo_a.shape),
                  _const_spec(g1.shape), _const_spec(g2.shape), _const_spec(g3.shape),
                  _const_spec(wgu.shape), _const_spec(wd.shape)],
        out_specs=row(D_MODEL),
        scratch_shapes=[pltpu.VMEM((tm, D_MODEL), F32)],
        compiler_params=pltpu.CompilerParams(dimension_semantics=("arbitrary",),
                                             vmem_limit_bytes=VMEM_LIMIT),
        name="out_ffn",
    )(ret, att, x2, wo_r, wo_a, g1, g2, g3, wgu, wd)


def _layer(x2, pos2, batch, seq, w_in, w_out, g_pre_mix, g_post_mix, g_pre_ffn, g_post_ffn,
           w_gate, w_up, w_down, tm):
    row = lambda g: g.reshape(1, D_MODEL).astype(F32)
    rq, rk, rv, rg, aq, ak, av = _in_proj(x2, pos2, row(g_pre_mix), w_in.astype(BF16), tm)
    ret = _retention(rq, rk, rv, rg, batch, seq)
    att = _dilated(aq, ak, av, batch, seq)
    w_out_bf = w_out.astype(BF16)
    wg = w_gate.astype(BF16).reshape(D_MODEL, N_FF_CHUNKS, FF_CHUNK)
    wu = w_up.astype(BF16).reshape(D_MODEL, N_FF_CHUNKS, FF_CHUNK)
    wgu = jnp.concatenate([wg, wu], axis=-1).transpose(1, 0, 2)
    wd = w_down.astype(BF16).reshape(N_FF_CHUNKS, FF_CHUNK, D_MODEL)
    return _out_ffn(ret, att, x2, w_out_bf[:RET_V_W], w_out_bf[RET_V_W:], row(g_post_mix),
                    row(g_pre_ffn), row(g_post_ffn), wgu, wd, tm)


def kernel(x, positions, w_in, w_out, g_pre_mix, g_post_mix, g_pre_ffn, g_post_ffn, w_gate, w_up, w_down):
    batch, seq, d = x.shape
    assert d == D_MODEL and seq % (ATT_BLOCK * DILATED_PATTERNS[-1][1]) == 0
    tm = 512
    x2 = x.reshape(batch * seq, d)
    pos2 = positions.reshape(batch * seq, 1)
    for l in range(w_in.shape[0]):
        x2 = _layer(x2, pos2, batch, seq, w_in[l], w_out[l], g_pre_mix[l], g_post_mix[l], g_pre_ffn[l],
                    g_post_ffn[l], w_gate[l], w_up[l], w_down[l], tm)
    return x2.reshape(batch, seq, d)
```

```python
import functools

import numpy as np
import jax
import jax.numpy as jnp
from jax import lax
from jax.experimental import pallas as pl
from jax.experimental.pallas import tpu as pltpu

F32 = jnp.float32
BF16 = jnp.bfloat16

D_MODEL = 1024
RET_HEADS = 8
RET_QK = 32
RET_V = 64
RET_QK_W = RET_HEADS * RET_QK
RET_V_W = RET_HEADS * RET_V
RET_CHUNK = 128
RET_ROT_BASE = 10000.0
ATT_HEADS = 8
ATT_DIM = 64
ATT_W = ATT_HEADS * ATT_DIM
ATT_BLOCK = 128
DILATED_PATTERNS = ((128, 1), (512, 4), (2048, 16))
ROPE_THETA = 500000.0
ROPE_DIM = ATT_DIM // 4
D_FF = 2816
RMS_EPS = 1e-6
GN_EPS = 1e-5
PROJ_WIDTH = 2 * RET_QK_W + 2 * RET_V_W + 3 * ATT_W

LANES = 128
FF_CHUNK = 256
N_FF_CHUNKS = D_FF // FF_CHUNK
IN_PROJ_ROWS = 1024
IN_PROJ_SUB_ROWS = 512
FFN_ROWS = 1024
FFN_SUB_ROWS = 512
VMEM_LIMIT = 56 * 1024 * 1024
LOG2E = 1.4426950408889634
RET_UNROLL = 4
STAGE_DIL = 4

TRIG_LANES = 32
TRIG_PACK = LANES // TRIG_LANES
TRIG_ONE_LANE = 24


def _rms(x, g):
    return x * lax.rsqrt(jnp.mean(x * x, axis=-1, keepdims=True) + RMS_EPS) * g


def _rot_tables():
    ret_f = 1.0 / (RET_ROT_BASE ** jnp.linspace(0.0, 1.0, RET_QK // 2, dtype=F32))
    att_f = ROPE_THETA ** (-jnp.arange(0, ROPE_DIM, 2, dtype=F32) / ROPE_DIM)
    tok = jnp.concatenate([ret_f, att_f, jnp.zeros((TRIG_LANES - 24,), F32)])
    freq = jnp.stack([jnp.zeros((LANES,), F32).at[j * TRIG_LANES:(j + 1) * TRIG_LANES].set(tok)
                      for j in range(TRIG_PACK)])
    e = np.zeros((TRIG_PACK, 2 * LANES, 4 * LANES), np.float32)
    for j in range(TRIG_PACK):
        cos0, sin0 = j * TRIG_LANES, LANES + j * TRIG_LANES
        for c in range(LANES):
            d = c % RET_QK
            e[j, cos0 + d % 16, c] = 1.0
            e[j, sin0 + d % 16, LANES + c] = -1.0 if d < 16 else 1.0
            d = c % ATT_DIM
            if d < ROPE_DIM:
                e[j, cos0 + 16 + d % 8, 2 * LANES + c] = 1.0
                e[j, sin0 + 16 + d % 8, 3 * LANES + c] = -1.0 if d < 8 else 1.0
            else:
                e[j, cos0 + TRIG_ONE_LANE, 2 * LANES + c] = 1.0
    return freq, jnp.asarray(e, BF16)


def _rotate(x, cos, sin, half, group):
    lane = lax.broadcasted_iota(jnp.int32, x.shape, 1)
    up = pltpu.roll(x, LANES - half, 1)
    dn = pltpu.roll(x, half, 1)
    partner = jnp.where((lane % group) < half, up, dn)
    return x * cos + partner * sin


def _in_proj_kernel(x_ref, pos_ref, g_ref, w_ref, freq_ref, e_ref,
                    rq_ref, rk_ref, rv_ref, rg_ref, aq_ref, ak_ref, av_ref):
    n_sub = x_ref.shape[0] // IN_PROJ_SUB_ROWS
    col = {}
    c0 = 0
    for name, width in (("rq", RET_QK_W), ("rk", RET_QK_W), ("rv", RET_V_W), ("rg", RET_V_W),
                        ("aq", ATT_W), ("ak", ATT_W), ("av", ATT_W)):
        col[name] = (c0, width)
        c0 += width

    def tables(s):
        blk = IN_PROJ_SUB_ROWS // TRIG_PACK
        ang = None
        for j in range(TRIG_PACK):
            pj = pos_ref[pl.ds(s * IN_PROJ_SUB_ROWS + j * blk, blk), :].astype(F32)
            term = pj * freq_ref[j:j + 1, :]
            ang = term if ang is None else ang + term
        cs = jnp.concatenate([jnp.cos(ang), jnp.sin(ang)], axis=1)
        hi = cs.astype(BF16)
        lo = (cs - hi.astype(F32)).astype(BF16)
        return jnp.concatenate(
            [jnp.dot(hi, e_ref[j], preferred_element_type=F32) + jnp.dot(lo, e_ref[j], preferred_element_type=F32)
             for j in range(TRIG_PACK)], axis=0)

    def body(s):
        rows = pl.ds(s * IN_PROJ_SUB_ROWS, IN_PROJ_SUB_ROWS)
        h = _rms(x_ref[rows, :], g_ref[...]).astype(BF16)

        def proj(name):
            c, width = col[name]
            return jnp.dot(h, w_ref[:, c:c + width], preferred_element_type=F32)

        rv_ref[rows, :] = proj("rv").astype(BF16)
        rg_ref[rows, :] = proj("rg").astype(BF16)
        av_ref[rows, :] = proj("av")
        tab = tables(s)
        cos_r, sin_r = tab[:, 0:LANES], tab[:, LANES:2 * LANES]
        cos_a, sin_a = tab[:, 2 * LANES:3 * LANES], tab[:, 3 * LANES:]
        for name, ref, cos, sin, half, group, scale in (
                ("rq", rq_ref, cos_r, sin_r, RET_QK // 2, RET_QK, None),
                ("rk", rk_ref, cos_r, sin_r, RET_QK // 2, RET_QK, RET_QK ** -0.5),
                ("aq", aq_ref, cos_a, sin_a, ROPE_DIM // 2, ATT_DIM, ATT_DIM ** -0.5 * LOG2E),
                ("ak", ak_ref, cos_a, sin_a, ROPE_DIM // 2, ATT_DIM, None)):
            p = proj(name)
            for j in range(p.shape[1] // LANES):
                r = _rotate(p[:, j * LANES:(j + 1) * LANES], cos, sin, half, group)
                if scale is not None:
                    r = r * scale
                ref[rows, j * LANES:(j + 1) * LANES] = r.astype(ref.dtype)

    for s in range(n_sub):
        body(s)


def _const_spec(shape):
    return pl.BlockSpec(shape, lambda *_: (0,) * len(shape), pipeline_mode=pl.Buffered(1))


def _in_proj(x2, pos2, g, w_bf, tm):
    t = x2.shape[0]
    freq, e = _rot_tables()
    row = lambda w: pl.BlockSpec((tm, w), lambda i: (i, 0))
    out_shape = (
        jax.ShapeDtypeStruct((t, RET_QK_W), BF16), jax.ShapeDtypeStruct((t, RET_QK_W), BF16),
        jax.ShapeDtypeStruct((t, RET_V_W), BF16), jax.ShapeDtypeStruct((t, RET_V_W), BF16),
        jax.ShapeDtypeStruct((t, ATT_W), F32), jax.ShapeDtypeStruct((t, ATT_W), F32),
        jax.ShapeDtypeStruct((t, ATT_W), F32))
    return pl.pallas_call(
        _in_proj_kernel,
        out_shape=out_shape,
        grid=(t // tm,),
        in_specs=[row(D_MODEL), row(1), _const_spec((1, D_MODEL)), _const_spec((D_MODEL, PROJ_WIDTH)),
                  _const_spec(freq.shape), _const_spec(e.shape)],
        out_specs=(row(RET_QK_W), row(RET_QK_W), row(RET_V_W), row(RET_V_W),
                   row(ATT_W), row(ATT_W), row(ATT_W)),
        compiler_params=pltpu.CompilerParams(dimension_semantics=("arbitrary",),
                                             vmem_limit_bytes=VMEM_LIMIT),
        name="in_proj",
    )(x2, pos2, g, w_bf, freq, e)


def _retention_consts():
    h, c = RET_HEADS, RET_CHUNK
    f = np.float32
    log_g = np.log1p(-np.exp2(f(-5.0) - np.arange(h, dtype=f))).astype(f)
    idx = np.arange(c, dtype=f)
    diff = idx[:, None] - idx[None, :]
    inner = np.where(diff >= 0, np.exp(log_g[:, None, None] * np.maximum(diff, f(0.0))), f(0.0)).astype(f)
    q_decay = np.exp(log_g[:, None] * (idx + f(1.0))[None, :]).astype(f)
    k_decay = np.exp(log_g[:, None] * (f(c) - f(1.0) - idx)[None, :]).astype(f)
    chunk_decay = np.exp(log_g * f(c)).astype(f)
    dec = np.concatenate([inner[0::2], inner[1::2]], axis=-1)
    qd = np.repeat(q_decay.T, RET_V, axis=1)
    kd = np.repeat(k_decay.T, RET_QK, axis=1)
    cd = np.repeat(chunk_decay, RET_V)[None, :]
    rows = np.arange(RET_QK_W)[:, None] // RET_QK
    cols = np.arange(RET_V_W)[None, :] // RET_V
    bd = (rows == cols).astype(f)
    gi = np.arange(2 * LANES) // RET_V
    gn = jnp.asarray(gi[:, None] == gi[None, :], BF16)
    return tuple(jnp.asarray(a) for a in (dec, qd, kd, cd, bd)) + (gn,)


def _retention_kernel(q_ref, k_ref, v_ref, g_ref, dec_ref, qd_ref, kd_ref, cd_ref, bd_ref, gn_ref,
                      o_ref, state_ref):
    n_chunks = q_ref.shape[0] // RET_CHUNK
    state_ref[...] = jnp.zeros_like(state_ref)
    lane_qk = lax.broadcasted_iota(jnp.int32, (RET_CHUNK, RET_QK_W), 1)
    lane_v = lax.broadcasted_iota(jnp.int32, (RET_CHUNK, LANES), 1)
    nt = (((1,), (1,)), ((), ()))
    tn = (((0,), (0,)), ((), ()))

    def group(i, carry):
        starts = [pl.multiple_of((i * RET_UNROLL + c) * RET_CHUNK, RET_CHUNK) for c in range(RET_UNROLL)]
        qs = [q_ref[pl.ds(r0, RET_CHUNK), :] for r0 in starts]
        ks = [k_ref[pl.ds(r0, RET_CHUNK), :] for r0 in starts]
        vs = [v_ref[pl.ds(r0, RET_CHUNK), :] for r0 in starts]
        pairs = range(RET_HEADS // 2)

        def two_heads_k(k, j):
            lo = 2 * j * RET_QK
            zero = jnp.zeros_like(k)
            return jnp.concatenate(
                [jnp.where((lane_qk >= lo) & (lane_qk < lo + RET_QK), k, zero),
                 jnp.where((lane_qk >= lo + RET_QK) & (lane_qk < lo + 2 * RET_QK), k, zero)], axis=0)

        def two_heads_v(v, j):
            vj = v[:, j * LANES:(j + 1) * LANES]
            zero = jnp.zeros_like(vj)
            return jnp.concatenate([jnp.where(lane_v < RET_V, vj, zero),
                                    jnp.where(lane_v >= RET_V, vj, zero)], axis=0)

        scores = [[lax.dot_general(q, two_heads_k(k, j), nt, preferred_element_type=F32) for j in pairs]
                  for q, k in zip(qs, ks)]
        kvs = [lax.dot_general((k.astype(F32) * kd_ref[...]).astype(BF16), v, tn, preferred_element_type=F32)
               for k, v in zip(ks, vs)]
        inners = [jnp.concatenate(
            [jnp.dot((sc[j] * dec_ref[j]).astype(BF16), two_heads_v(v, j), preferred_element_type=F32)
             for j in pairs], axis=1) for sc, v in zip(scores, vs)]
        state = state_ref[...]
        outs = []
        for q, kv, inner in zip(qs, kvs, inners):
            cross = jnp.dot(q, state.astype(BF16), preferred_element_type=F32) * qd_ref[...]
            outs.append(inner + cross)
            state = state * cd_ref[...] + kv * bd_ref[...]
        state_ref[...] = state
        gn = gn_ref[...]
        halves = [[o[:, t * 2 * LANES:(t + 1) * 2 * LANES] for t in range(2)] for o in outs]
        mus = [[jnp.dot(o_t.astype(BF16), gn, preferred_element_type=F32) * (1.0 / RET_V) for o_t in hv]
               for hv in halves]
        ds = [[o_t - mu for o_t, mu in zip(hv, mv)] for hv, mv in zip(halves, mus)]
        vars_ = [[jnp.dot((d * d).astype(BF16), gn, preferred_element_type=F32) * (1.0 / RET_V) for d in dv]
                 for dv in ds]
        for r0, dv, vv in zip(starts, ds, vars_):
            y = jnp.concatenate([d * lax.rsqrt(var + GN_EPS) for d, var in zip(dv, vv)], axis=1)
            g = g_ref[pl.ds(r0, RET_CHUNK), :].astype(F32)
            o_ref[pl.ds(r0, RET_CHUNK), :] = (y * (g * jax.nn.sigmoid(g))).astype(BF16)
        return carry

    assert n_chunks % RET_UNROLL == 0
    lax.fori_loop(0, n_chunks // RET_UNROLL, group, 0)


def _retention(rq, rk, rv, rg, batch, seq):
    consts = _retention_consts()
    seq_spec = lambda w: pl.BlockSpec((seq, w), lambda b: (b, 0))
    return pl.pallas_call(
        _retention_kernel,
        out_shape=jax.ShapeDtypeStruct((batch * seq, RET_V_W), BF16),
        grid=(batch,),
        in_specs=[seq_spec(RET_QK_W), seq_spec(RET_QK_W), seq_spec(RET_V_W), seq_spec(RET_V_W)]
                 + [_const_spec(c.shape) for c in consts],
        out_specs=seq_spec(RET_V_W),
        scratch_shapes=[pltpu.VMEM((RET_QK_W, RET_V_W), F32)],
        compiler_params=pltpu.CompilerParams(dimension_semantics=("arbitrary",),
                                             vmem_limit_bytes=VMEM_LIMIT),
        name="retention",
    )(rq, rk, rv, rg, *consts)


def _attend(qb, kb, vb, bias):
    nk = kb.shape[0]
    lane = lax.broadcasted_iota(jnp.int32, (ATT_BLOCK, LANES), 1)
    first_head = lane < ATT_DIM
    kbf = kb.astype(BF16)
    vext = jnp.concatenate([vb.astype(BF16), jnp.ones((nk, LANES), BF16)], axis=1)
    res = []
    for head_lanes in (first_head, jnp.logical_not(first_head)):
        qm = jnp.where(head_lanes, qb, 0.0).astype(BF16)
        s = lax.dot_general(qm, kbf, (((1,), (1,)), ((), ())), preferred_element_type=F32) + bias
        m = jnp.max(s, axis=-1, keepdims=True)
        p = jnp.exp2(s - m)
        pv = jnp.dot(p.astype(BF16), vext, preferred_element_type=F32)
        res.append((pv[:, :LANES], pv[:, LANES:], m))
    return tuple(jnp.where(first_head, a, b) for a, b in zip(*res))


def _attend_first(qb, kb, vb, bias2):
    lane = lax.broadcasted_iota(jnp.int32, (ATT_BLOCK, LANES), 1)
    first_head = lane < ATT_DIM
    sel_a = lambda x: jnp.where(first_head, x, 0.0).astype(BF16)
    sel_b = lambda x: jnp.where(first_head, 0.0, x).astype(BF16)
    one = jnp.ones((ATT_BLOCK, LANES), F32)
    kcat = jnp.concatenate([sel_a(kb), sel_b(kb)], axis=0)
    s = lax.dot_general(qb.astype(BF16), kcat, (((1,), (1,)), ((), ())), preferred_element_type=F32) + bias2
    ma = jnp.max(s[:, :ATT_BLOCK], axis=-1, keepdims=True)
    mb = jnp.max(s[:, ATT_BLOCK:], axis=-1, keepdims=True)
    p = jnp.concatenate([jnp.exp2(s[:, :ATT_BLOCK] - ma), jnp.exp2(s[:, ATT_BLOCK:] - mb)], axis=1)
    vcat = jnp.concatenate([jnp.concatenate([sel_a(vb), sel_a(one)], axis=1),
                            jnp.concatenate([sel_b(vb), sel_b(one)], axis=1)], axis=0)
    pv = jnp.dot(p.astype(BF16), vcat, preferred_element_type=F32)
    return pv[:, :LANES], pv[:, LANES:], jnp.where(first_head, ma, mb)


def _dilated_kernel(q_ref, k_ref, v_ref, o_ref, bias_ref, bias2_ref, q4_ref, k4_ref, v4_ref, *scratch):
    seq = q_ref.shape[0]
    blk = ATT_BLOCK
    sd = STAGE_DIL
    lc = seq // sd
    a = lax.broadcasted_iota(jnp.int32, (blk, 2 * blk), 0)
    kk = lax.broadcasted_iota(jnp.int32, (blk, 2 * blk), 1)
    dist = blk + a - kk
    bias_ref[...] = jnp.where((dist >= 0) & (dist <= blk), 0.0, -jnp.inf)
    causal = jnp.where(kk % blk <= a, 0.0, -jnp.inf)
    bias2_ref[...] = causal
    for r in range(sd):
        for src, dst in ((q_ref, q4_ref), (k_ref, k4_ref), (v_ref, v4_ref)):
            dst[r * lc:(r + 1) * lc, :] = src[pl.ds(r, lc, stride=sd), :]

    def view(p_idx, cls, b, n):
        dil = DILATED_PATTERNS[p_idx][1]
        if dil == 1:
            return (q_ref, k_ref, v_ref), pl.ds(b * blk, n)
        if dil == sd:
            return (q4_ref, k4_ref, v4_ref), pl.ds(cls * lc + b * blk, n)
        return (q4_ref, k4_ref, v4_ref), pl.ds((cls % sd) * lc + cls // sd + b * blk * sd, n, stride=sd)

    def run(tasks, first):
        loaded = []
        for p_idx, cls, b in tasks:
            (qr, kr, vr), qs = view(p_idx, cls, b, blk)
            _, ks = (None, qs) if first else view(p_idx, cls, b - 1, 2 * blk)
            loaded.append((qr[qs, :], kr[ks, :], vr[ks, :], qs))
        if first:
            bias2 = bias2_ref[...]
            results = [_attend_first(qb, kb, vb, bias2) for qb, kb, vb, _ in loaded]
        else:
            bias = bias_ref[...]
            results = [_attend(qb, kb, vb, bias) for qb, kb, vb, _ in loaded]
        for (p_idx, _, _), (_, _, _, dst), res in zip(tasks, loaded, results):
            for ref, val in zip(scratch[3 * p_idx:3 * p_idx + 3], res):
                ref[dst, :] = val

    n_blocks = [seq // dil // blk for _, dil in DILATED_PATTERNS]
    run([(p, cls, 0) for p, (_, dil) in enumerate(DILATED_PATTERNS) for cls in range(dil)], True)
    for p, (_, dil) in enumerate(DILATED_PATTERNS):
        if n_blocks[p] > 1:
            run([(p, cls, b) for b in range(1, n_blocks[p]) for cls in range(dil)], False)

    def combine(c, carry):
        for r in range(sd):
            nat = pl.ds(r + c * blk * sd, blk, stride=sd)
            cls = pl.ds(pl.multiple_of(r * lc + c * blk, blk), blk)
            sls = [nat if dil == 1 else cls for _, dil in DILATED_PATTERNS]
            ms = [scratch[3 * p + 2][sl, :] for p, sl in enumerate(sls)]
            m = functools.reduce(jnp.maximum, ms)
            ws = [jnp.exp2(mp - m) for mp in ms]
            num = functools.reduce(lambda x, y: x + y,
                                   [w * scratch[3 * p][sl, :] for p, (w, sl) in enumerate(zip(ws, sls))])
            den = functools.reduce(lambda x, y: x + y,
                                   [w * scratch[3 * p + 1][sl, :] for p, (w, sl) in enumerate(zip(ws, sls))])
            o_ref[nat, :] = num / den
        return carry

    lax.fori_loop(0, lc // blk, combine, 0)


def _dilated(aq, ak, av, batch, seq):
    dils = [dil for _, dil in DILATED_PATTERNS]
    assert dils == [1, STAGE_DIL, STAGE_DIL * STAGE_DIL]
    for window, dil in DILATED_PATTERNS:
        assert window // dil == ATT_BLOCK and (seq // dil) % ATT_BLOCK == 0
    spec = pl.BlockSpec((seq, LANES), lambda b, hp: (b, hp))
    return pl.pallas_call(
        _dilated_kernel,
        out_shape=jax.ShapeDtypeStruct((batch * seq, ATT_W), F32),
        grid=(batch, ATT_W // LANES),
        in_specs=[spec, spec, spec],
        out_specs=spec,
        scratch_shapes=[pltpu.VMEM((ATT_BLOCK, 2 * ATT_BLOCK), F32) for _ in range(2)]
                       + [pltpu.VMEM((seq, LANES), F32) for _ in range(3 + 3 * len(DILATED_PATTERNS))],
        compiler_params=pltpu.CompilerParams(dimension_semantics=("arbitrary", "arbitrary"),
                                             vmem_limit_bytes=VMEM_LIMIT),
        name="dilated",
    )(aq, ak, av)


def _out_ffn_kernel(ret_ref, att_ref, x_ref, wo_ref, g1_ref, g2_ref, g3_ref, wg_ref, wu_ref, wd_ref,
                    o_ref):
    n_sub = x_ref.shape[0] // FFN_SUB_ROWS

    def head(s):
        rows = pl.ds(s * FFN_SUB_ROWS, FFN_SUB_ROWS)
        mix = (jnp.dot(ret_ref[rows, :], wo_ref[:RET_V_W, :], preferred_element_type=F32)
               + jnp.dot(att_ref[rows, :].astype(BF16), wo_ref[RET_V_W:, :], preferred_element_type=F32))
        x1 = x_ref[rows, :] + _rms(mix, g1_ref[...])
        return x1, _rms(x1, g2_ref[...]).astype(BF16)

    def swiglu(h):
        acc = None
        for c in range(N_FF_CHUNKS):
            cols = slice(c * FF_CHUNK, (c + 1) * FF_CHUNK)
            a = jnp.dot(h, wg_ref[:, cols], preferred_element_type=F32)
            u = jnp.dot(h, wu_ref[:, cols], preferred_element_type=F32)
            act = (a * jax.nn.sigmoid(a) * u).astype(BF16)
            d = jnp.dot(act, wd_ref[cols, :], preferred_element_type=F32)
            acc = d if acc is None else acc + d
        return acc

    cur = head(0)
    for s in range(n_sub):
        nxt = head(s + 1) if s + 1 < n_sub else None
        x1, h = cur
        o_ref[pl.ds(s * FFN_SUB_ROWS, FFN_SUB_ROWS), :] = x1 + _rms(swiglu(h), g3_ref[...])
        cur = nxt


def _out_ffn(ret, att, x2, wo, g1, g2, g3, wg, wu, wd, tm):
    t = x2.shape[0]
    row = lambda w: pl.BlockSpec((tm, w), lambda i: (i, 0))
    consts = (wo, g1, g2, g3, wg, wu, wd)
    return pl.pallas_call(
        _out_ffn_kernel,
        out_shape=jax.ShapeDtypeStruct((t, D_MODEL), F32),
        grid=(t // tm,),
        in_specs=[row(RET_V_W), row(ATT_W), row(D_MODEL)] + [_const_spec(c.shape) for c in consts],
        out_specs=row(D_MODEL),
        compiler_params=pltpu.CompilerParams(dimension_semantics=("arbitrary",),
                                             vmem_limit_bytes=VMEM_LIMIT),
        name="out_ffn",
    )(ret, att, x2, *consts)


def _layer(x2, pos2, batch, seq, w_in, w_out, g_pre_mix, g_post_mix, g_pre_ffn, g_post_ffn,
           w_gate, w_up, w_down):
    row = lambda g: g.reshape(1, D_MODEL).astype(F32)
    rq, rk, rv, rg, aq, ak, av = _in_proj(x2, pos2, row(g_pre_mix), w_in.astype(BF16), IN_PROJ_ROWS)
    ret = _retention(rq, rk, rv, rg, batch, seq)
    att = _dilated(aq, ak, av, batch, seq)
    return _out_ffn(ret, att, x2, w_out.astype(BF16), row(g_post_mix), row(g_pre_ffn), row(g_post_ffn),
                    w_gate.astype(BF16), w_up.astype(BF16), w_down.astype(BF16), FFN_ROWS)


def kernel(x, positions, w_in, w_out, g_pre_mix, g_post_mix, g_pre_ffn, g_post_ffn, w_gate, w_up, w_down):
    batch, seq, d = x.shape
    assert d == D_MODEL and seq % (ATT_BLOCK * DILATED_PATTERNS[-1][1]) == 0
    assert (batch * seq) % FFN_ROWS == 0 and (batch * seq) % IN_PROJ_ROWS == 0
    x2 = x.reshape(batch * seq, d)
    pos2 = positions.reshape(batch * seq, 1)
    for l in range(w_in.shape[0]):
        x2 = _layer(x2, pos2, batch, seq, w_in[l], w_out[l], g_pre_mix[l], g_post_mix[l], g_pre_ffn[l],
                    g_post_ffn[l], w_gate[l], w_up[l], w_down[l])
    return x2.reshape(batch, seq, d)
```

```python
import functools

import numpy as np
import jax
import jax.numpy as jnp
from jax import lax
from jax.experimental import pallas as pl
from jax.experimental.pallas import tpu as pltpu

F32 = jnp.float32
BF16 = jnp.bfloat16

D_MODEL = 1024
RET_HEADS = 8
RET_QK = 32
RET_V = 64
RET_QK_W = RET_HEADS * RET_QK
RET_V_W = RET_HEADS * RET_V
RET_CHUNK = 128
RET_ROT_BASE = 10000.0
ATT_HEADS = 8
ATT_DIM = 64
ATT_W = ATT_HEADS * ATT_DIM
ATT_BLOCK = 128
DILATED_PATTERNS = ((128, 1), (512, 4), (2048, 16))
ROPE_THETA = 500000.0
ROPE_DIM = ATT_DIM // 4
D_FF = 2816
RMS_EPS = 1e-6
GN_EPS = 1e-5
PROJ_WIDTH = 2 * RET_QK_W + 2 * RET_V_W + 3 * ATT_W

LANES = 128
FF_CHUNK = 256
N_FF_CHUNKS = D_FF // FF_CHUNK
IN_PROJ_ROWS = 1024
IN_PROJ_SUB_ROWS = 512
FFN_ROWS = 1024
FFN_SUB_ROWS = 512
VMEM_LIMIT = 56 * 1024 * 1024
LOG2E = 1.4426950408889634
RET_GROUP = 8
STAGE_DIL = 4

TRIG_LANES = 32
TRIG_PACK = LANES // TRIG_LANES
TRIG_ONE_LANE = 24


def _rms(x, g):
    return x * lax.rsqrt(jnp.mean(x * x, axis=-1, keepdims=True) + RMS_EPS) * g


def _rot_tables():
    ret_f = 1.0 / (RET_ROT_BASE ** jnp.linspace(0.0, 1.0, RET_QK // 2, dtype=F32))
    att_f = ROPE_THETA ** (-jnp.arange(0, ROPE_DIM, 2, dtype=F32) / ROPE_DIM)
    tok = jnp.concatenate([ret_f, att_f, jnp.zeros((TRIG_LANES - 24,), F32)])
    freq = jnp.stack([jnp.zeros((LANES,), F32).at[j * TRIG_LANES:(j + 1) * TRIG_LANES].set(tok)
                      for j in range(TRIG_PACK)])
    e = np.zeros((TRIG_PACK, 2 * LANES, 4 * LANES), np.float32)
    for j in range(TRIG_PACK):
        cos0, sin0 = j * TRIG_LANES, LANES + j * TRIG_LANES
        for c in range(LANES):
            d = c % RET_QK
            e[j, cos0 + d % 16, c] = 1.0
            e[j, sin0 + d % 16, LANES + c] = -1.0 if d < 16 else 1.0
            d = c % ATT_DIM
            if d < ROPE_DIM:
                e[j, cos0 + 16 + d % 8, 2 * LANES + c] = 1.0
                e[j, sin0 + 16 + d % 8, 3 * LANES + c] = -1.0 if d < 8 else 1.0
            else:
                e[j, cos0 + TRIG_ONE_LANE, 2 * LANES + c] = 1.0
    return freq, jnp.asarray(e, BF16)


def _rotate(x, cos, sin, half, group):
    lane = lax.broadcasted_iota(jnp.int32, x.shape, 1)
    up = pltpu.roll(x, LANES - half, 1)
    dn = pltpu.roll(x, half, 1)
    partner = jnp.where((lane % group) < half, up, dn)
    return x * cos + partner * sin


def _in_proj_kernel(x_ref, pos_ref, g_ref, w_ref, freq_ref, e_ref, qdec_ref, kdec_ref,
                    rq_ref, rqd_ref, rkt_ref, rkdt_ref, rv_ref, rg_ref, aq_ref, ak_ref, av_ref):
    n_sub = x_ref.shape[0] // IN_PROJ_SUB_ROWS
    col = {}
    c0 = 0
    for name, width in (("rq", RET_QK_W), ("rk", RET_QK_W), ("rv", RET_V_W), ("rg", RET_V_W),
                        ("aq", ATT_W), ("ak", ATT_W), ("av", ATT_W)):
        col[name] = (c0, width)
        c0 += width

    def tables(s):
        blk = IN_PROJ_SUB_ROWS // TRIG_PACK
        ang = None
        for j in range(TRIG_PACK):
            pj = pos_ref[pl.ds(s * IN_PROJ_SUB_ROWS + j * blk, blk), :].astype(F32)
            term = pj * freq_ref[j:j + 1, :]
            ang = term if ang is None else ang + term
        cs = jnp.concatenate([jnp.cos(ang), jnp.sin(ang)], axis=1)
        hi = cs.astype(BF16)
        lo = (cs - hi.astype(F32)).astype(BF16)
        return jnp.concatenate(
            [jnp.dot(hi, e_ref[j], preferred_element_type=F32) + jnp.dot(lo, e_ref[j], preferred_element_type=F32)
             for j in range(TRIG_PACK)], axis=0)

    def body(s):
        rows = pl.ds(s * IN_PROJ_SUB_ROWS, IN_PROJ_SUB_ROWS)
        h = _rms(x_ref[rows, :], g_ref[...]).astype(BF16)

        def proj(name):
            c, width = col[name]
            return jnp.dot(h, w_ref[:, c:c + width], preferred_element_type=F32)

        rv_ref[rows, :] = proj("rv").astype(BF16)
        rg_ref[rows, :] = proj("rg").astype(BF16)
        av_ref[rows, :] = proj("av")
        tab = tables(s)
        cos_r, sin_r = tab[:, 0:LANES], tab[:, LANES:2 * LANES]
        cos_a, sin_a = tab[:, 2 * LANES:3 * LANES], tab[:, 3 * LANES:]
        for name, ref, cos, sin, half, group, scale, dec_ref, decayed_ref, transposed in (
                ("rq", rq_ref, cos_r, sin_r, RET_QK // 2, RET_QK, None, qdec_ref, rqd_ref, False),
                ("rk", rkt_ref, cos_r, sin_r, RET_QK // 2, RET_QK, RET_QK ** -0.5, kdec_ref, rkdt_ref, True),
                ("aq", aq_ref, cos_a, sin_a, ROPE_DIM // 2, ATT_DIM, ATT_DIM ** -0.5 * LOG2E, None, None, False),
                ("ak", ak_ref, cos_a, sin_a, ROPE_DIM // 2, ATT_DIM, None, None, None, False)):
            p = proj(name)
            for j in range(p.shape[1] // LANES):
                lanes = slice(j * LANES, (j + 1) * LANES)
                r = _rotate(p[:, lanes], cos, sin, half, group)
                if scale is not None:
                    r = r * scale
                outs = [(ref, r)] + ([(decayed_ref, r * dec_ref[:, lanes])] if dec_ref is not None else [])
                for dst, val in outs:
                    if transposed:
                        dst[lanes, rows] = val.T.astype(dst.dtype)
                    else:
                        dst[rows, lanes] = val.astype(dst.dtype)

    for s in range(n_sub):
        body(s)


def _const_spec(shape):
    return pl.BlockSpec(shape, lambda *_: (0,) * len(shape), pipeline_mode=pl.Buffered(1))


def _in_proj(x2, pos2, g, w_bf, tm):
    t = x2.shape[0]
    freq, e = _rot_tables()
    tables = _retention_tables()
    assert IN_PROJ_SUB_ROWS % RET_CHUNK == 0
    reps = IN_PROJ_SUB_ROWS // RET_CHUNK
    qdec, kdec = (jnp.asarray(np.tile(tables[n], (reps, 1))) for n in ("q_decay", "k_decay"))
    consts = (g, w_bf, freq, e, qdec, kdec)
    row = lambda w: pl.BlockSpec((tm, w), lambda i: (i, 0))
    outs = [((t, RET_QK_W), BF16, row(RET_QK_W))] * 2
    outs += [((RET_QK_W, t), BF16, pl.BlockSpec((RET_QK_W, tm), lambda i: (0, i)))] * 2
    outs += [((t, RET_V_W), BF16, row(RET_V_W))] * 2 + [((t, ATT_W), F32, row(ATT_W))] * 3
    return pl.pallas_call(
        _in_proj_kernel,
        out_shape=tuple(jax.ShapeDtypeStruct(shape, dt) for shape, dt, _ in outs),
        grid=(t // tm,),
        in_specs=[row(D_MODEL), row(1)] + [_const_spec(c.shape) for c in consts],
        out_specs=tuple(spec for _, _, spec in outs),
        compiler_params=pltpu.CompilerParams(dimension_semantics=("arbitrary",),
                                             vmem_limit_bytes=VMEM_LIMIT),
        name="in_proj",
    )(x2, pos2, *consts)


def _retention_tables():
    h, c = RET_HEADS, RET_CHUNK
    f = np.float32
    log_g = np.log1p(-np.exp2(f(-5.0) - np.arange(h, dtype=f))).astype(f)
    idx = np.arange(c, dtype=f)
    diff = idx[:, None] - idx[None, :]
    inner = np.where(diff >= 0, np.exp(log_g[:, None, None] * np.maximum(diff, f(0.0))), f(0.0)).astype(f)
    q_decay = np.exp(log_g[:, None] * (idx + f(1.0))[None, :]).astype(f)
    k_decay = np.exp(log_g[:, None] * (f(c) - f(1.0) - idx)[None, :]).astype(f)
    chunk_decay = np.exp(log_g * f(c)).astype(f)
    rows = np.arange(RET_QK_W)[:, None] // RET_QK
    cols = np.arange(RET_V_W)[None, :] // RET_V
    gi = np.arange(2 * LANES) // RET_V
    return {
        "inner_decay": np.concatenate([inner[0::2], inner[1::2]], axis=-1),
        "q_decay": np.repeat(q_decay.T, RET_QK, axis=1),
        "k_decay": np.repeat(k_decay.T, RET_QK, axis=1),
        "chunk_decay": np.repeat(chunk_decay, RET_V)[None, :],
        "state_mask": (rows == cols).astype(f),
        "group_mean": (gi[:, None] == gi[None, :]).astype(f) / f(RET_V),
    }


def _retention_kernel(q_ref, qd_ref, kt_ref, kdt_ref, v_ref, g_ref, dec_ref, cd_ref, bd_ref, gn_ref, o_ref):
    n_chunks = q_ref.shape[0] // RET_CHUNK
    lane_v = lax.broadcasted_iota(jnp.int32, (RET_CHUNK, LANES), 1)
    pairs = range(RET_HEADS // 2)

    def rows(ref, c):
        return ref[c * RET_CHUNK:(c + 1) * RET_CHUNK, :]

    def two_heads_kt(kt, j):
        def only(lo):
            pieces = [(lo, None), (RET_QK, kt[lo:lo + RET_QK, :]), (RET_QK_W - lo - RET_QK, None)]
            return jnp.concatenate([jnp.zeros((n, RET_CHUNK), BF16) if p is None else p
                                    for n, p in pieces if n > 0], axis=0)
        return jnp.concatenate([only(2 * j * RET_QK), only((2 * j + 1) * RET_QK)], axis=1)

    def two_heads_v(v, j):
        vj = v[:, j * LANES:(j + 1) * LANES]
        zero = jnp.zeros_like(vj)
        return jnp.concatenate([jnp.where(lane_v < RET_V, vj, zero),
                                jnp.where(lane_v >= RET_V, vj, zero)], axis=0)

    def stage_scores(chunks):
        return [[jnp.dot(rows(q_ref, c), two_heads_kt(kt_ref[:, c * RET_CHUNK:(c + 1) * RET_CHUNK], j),
                         preferred_element_type=F32) for j in pairs] for c in chunks]

    def stage_kv(chunks):
        return [jnp.dot(kdt_ref[:, c * RET_CHUNK:(c + 1) * RET_CHUNK], rows(v_ref, c),
                        preferred_element_type=F32) for c in chunks]

    def stage_pv(chunks, scores):
        return [jnp.concatenate(
            [jnp.dot((sc[j] * dec_ref[j]).astype(BF16), two_heads_v(rows(v_ref, c), j),
                     preferred_element_type=F32) for j in pairs], axis=1)
            for c, sc in zip(chunks, scores)]

    def stage_recurrence(chunks, kvs, inners, state):
        outs = []
        for c, kv, inner in zip(chunks, kvs, inners):
            outs.append(inner + jnp.dot(rows(qd_ref, c), state.astype(BF16), preferred_element_type=F32))
            state = state * cd_ref[...] + kv * bd_ref[...]
        return outs, state

    def halves(x):
        return [x[:, t * 2 * LANES:(t + 1) * 2 * LANES] for t in range(2)]

    def stage_mean(outs):
        return [[o_t - jnp.dot(o_t.astype(BF16), gn_ref[...], preferred_element_type=F32) for o_t in halves(o)]
                for o in outs]

    def stage_var(ds):
        return [[jnp.dot((d * d).astype(BF16), gn_ref[...], preferred_element_type=F32) for d in dv] for dv in ds]

    def stage_store(chunks, ds, vars_):
        for c, dv, vv in zip(chunks, ds, vars_):
            y = jnp.concatenate([d * lax.rsqrt(var + GN_EPS) for d, var in zip(dv, vv)], axis=1)
            g = rows(g_ref, c).astype(F32)
            o_ref[c * RET_CHUNK:(c + 1) * RET_CHUNK, :] = (y * (g * jax.nn.sigmoid(g))).astype(BF16)

    assert n_chunks % RET_GROUP == 0
    groups = [list(range(g0, g0 + RET_GROUP)) for g0 in range(0, n_chunks, RET_GROUP)]
    state = jnp.zeros((RET_QK_W, RET_V_W), F32)
    pending = None
    sc, kv = stage_scores(groups[0]), stage_kv(groups[0])
    for gi, chunks in enumerate(groups):
        inner = stage_pv(chunks, sc)
        if gi + 1 < len(groups):
            nxt = (stage_scores(groups[gi + 1]), stage_kv(groups[gi + 1]))
        outs, state = stage_recurrence(chunks, kv, inner, state)
        if pending is not None:
            stage_store(pending[0], pending[1], stage_var(pending[1]))
        pending = (chunks, stage_mean(outs))
        if gi + 1 < len(groups):
            sc, kv = nxt
    stage_store(pending[0], pending[1], stage_var(pending[1]))


def _retention(rq, rqd, rkt, rkdt, rv, rg, batch, seq):
    tables = _retention_tables()
    consts = tuple(jnp.asarray(tables[n]) for n in ("inner_decay", "chunk_decay", "state_mask"))
    consts += (jnp.asarray(tables["group_mean"], BF16),)
    seq_spec = lambda w: pl.BlockSpec((seq, w), lambda b: (b, 0))
    t_spec = pl.BlockSpec((RET_QK_W, seq), lambda b: (0, b))
    return pl.pallas_call(
        _retention_kernel,
        out_shape=jax.ShapeDtypeStruct((batch * seq, RET_V_W), BF16),
        grid=(batch,),
        in_specs=[seq_spec(RET_QK_W)] * 2 + [t_spec] * 2 + [seq_spec(RET_V_W)] * 2
                 + [_const_spec(c.shape) for c in consts],
        out_specs=seq_spec(RET_V_W),
        compiler_params=pltpu.CompilerParams(dimension_semantics=("arbitrary",),
                                             vmem_limit_bytes=VMEM_LIMIT),
        name="retention",
    )(rq, rqd, rkt, rkdt, rv, rg, *consts)


def _attend(qb, kb, vb, bias):
    nk = kb.shape[0]
    lane = lax.broadcasted_iota(jnp.int32, (ATT_BLOCK, LANES), 1)
    first_head = lane < ATT_DIM
    kbf = kb.astype(BF16)
    vext = jnp.concatenate([vb.astype(BF16), jnp.ones((nk, LANES), BF16)], axis=1)
    res = []
    for head_lanes in (first_head, jnp.logical_not(first_head)):
        qm = jnp.where(head_lanes, qb, 0.0).astype(BF16)
        s = lax.dot_general(qm, kbf, (((1,), (1,)), ((), ())), preferred_element_type=F32) + bias
        m = jnp.max(s, axis=-1, keepdims=True)
        p = jnp.exp2(s - m)
        pv = jnp.dot(p.astype(BF16), vext, preferred_element_type=F32)
        res.append((pv[:, :LANES], pv[:, LANES:], m))
    return tuple(jnp.where(first_head, a, b) for a, b in zip(*res))


def _attend_first(qb, kb, vb, bias2):
    lane = lax.broadcasted_iota(jnp.int32, (ATT_BLOCK, LANES), 1)
    first_head = lane < ATT_DIM
    sel_a = lambda x: jnp.where(first_head, x, 0.0).astype(BF16)
    sel_b = lambda x: jnp.where(first_head, 0.0, x).astype(BF16)
    one = jnp.ones((ATT_BLOCK, LANES), F32)
    kcat = jnp.concatenate([sel_a(kb), sel_b(kb)], axis=0)
    s = lax.dot_general(qb.astype(BF16), kcat, (((1,), (1,)), ((), ())), preferred_element_type=F32) + bias2
    ma = jnp.max(s[:, :ATT_BLOCK], axis=-1, keepdims=True)
    mb = jnp.max(s[:, ATT_BLOCK:], axis=-1, keepdims=True)
    p = jnp.concatenate([jnp.exp2(s[:, :ATT_BLOCK] - ma), jnp.exp2(s[:, ATT_BLOCK:] - mb)], axis=1)
    vcat = jnp.concatenate([jnp.concatenate([sel_a(vb), sel_a(one)], axis=1),
                            jnp.concatenate([sel_b(vb), sel_b(one)], axis=1)], axis=0)
    pv = jnp.dot(p.astype(BF16), vcat, preferred_element_type=F32)
    return pv[:, :LANES], pv[:, LANES:], jnp.where(first_head, ma, mb)


def _dilated_kernel(q_ref, k_ref, v_ref, o_ref, bias_ref, bias2_ref, q4_ref, k4_ref, v4_ref, *scratch):
    seq = q_ref.shape[0]
    blk = ATT_BLOCK
    sd = STAGE_DIL
    lc = seq // sd
    a = lax.broadcasted_iota(jnp.int32, (blk, 2 * blk), 0)
    kk = lax.broadcasted_iota(jnp.int32, (blk, 2 * blk), 1)
    dist = blk + a - kk
    bias_ref[...] = jnp.where((dist >= 0) & (dist <= blk), 0.0, -jnp.inf)
    causal = jnp.where(kk % blk <= a, 0.0, -jnp.inf)
    bias2_ref[...] = causal
    for r in range(sd):
        for src, dst in ((q_ref, q4_ref), (k_ref, k4_ref), (v_ref, v4_ref)):
            dst[r * lc:(r + 1) * lc, :] = src[pl.ds(r, lc, stride=sd), :]

    def view(p_idx, cls, b, n):
        dil = DILATED_PATTERNS[p_idx][1]
        if dil == 1:
            return (q_ref, k_ref, v_ref), pl.ds(b * blk, n)
        if dil == sd:
            return (q4_ref, k4_ref, v4_ref), pl.ds(cls * lc + b * blk, n)
        return (q4_ref, k4_ref, v4_ref), pl.ds((cls % sd) * lc + cls // sd + b * blk * sd, n, stride=sd)

    def run(tasks, first):
        loaded = []
        for p_idx, cls, b in tasks:
            (qr, kr, vr), qs = view(p_idx, cls, b, blk)
            _, ks = (None, qs) if first else view(p_idx, cls, b - 1, 2 * blk)
            loaded.append((qr[qs, :], kr[ks, :], vr[ks, :], qs))
        if first:
            bias2 = bias2_ref[...]
            results = [_attend_first(qb, kb, vb, bias2) for qb, kb, vb, _ in loaded]
        else:
            bias = bias_ref[...]
            results = [_attend(qb, kb, vb, bias) for qb, kb, vb, _ in loaded]
        for (p_idx, _, _), (_, _, _, dst), res in zip(tasks, loaded, results):
            for ref, val in zip(scratch[3 * p_idx:3 * p_idx + 3], res):
                ref[dst, :] = val

    n_blocks = [seq // dil // blk for _, dil in DILATED_PATTERNS]
    run([(p, cls, 0) for p, (_, dil) in enumerate(DILATED_PATTERNS) for cls in range(dil)], True)
    for p, (_, dil) in enumerate(DILATED_PATTERNS):
        if dil > 1 and n_blocks[p] > 1:
            run([(p, cls, b) for b in range(1, n_blocks[p]) for cls in range(dil)], False)

    for c in range(lc // blk):
        run([(0, 0, b) for b in range(max(1, c * sd), (c + 1) * sd)], False)
        for r in range(sd):
            nat = pl.ds(r + c * blk * sd, blk, stride=sd)
            cls = pl.ds(r * lc + c * blk, blk)
            sls = [nat if dil == 1 else cls for _, dil in DILATED_PATTERNS]
            ms = [scratch[3 * p + 2][sl, :] for p, sl in enumerate(sls)]
            m = functools.reduce(jnp.maximum, ms)
            ws = [jnp.exp2(mp - m) for mp in ms]
            num = functools.reduce(lambda x, y: x + y,
                                   [w * scratch[3 * p][sl, :] for p, (w, sl) in enumerate(zip(ws, sls))])
            den = functools.reduce(lambda x, y: x + y,
                                   [w * scratch[3 * p + 1][sl, :] for p, (w, sl) in enumerate(zip(ws, sls))])
            o_ref[nat, :] = num / den


def _dilated(aq, ak, av, batch, seq):
    dils = [dil for _, dil in DILATED_PATTERNS]
    assert dils == [1, STAGE_DIL, STAGE_DIL * STAGE_DIL]
    for window, dil in DILATED_PATTERNS:
        assert window // dil == ATT_BLOCK and (seq // dil) % ATT_BLOCK == 0
    spec = pl.BlockSpec((seq, LANES), lambda b, hp: (b, hp))
    return pl.pallas_call(
        _dilated_kernel,
        out_shape=jax.ShapeDtypeStruct((batch * seq, ATT_W), F32),
        grid=(batch, ATT_W // LANES),
        in_specs=[spec, spec, spec],
        out_specs=spec,
        scratch_shapes=[pltpu.VMEM((ATT_BLOCK, 2 * ATT_BLOCK), F32) for _ in range(2)]
                       + [pltpu.VMEM((seq, LANES), F32) for _ in range(3 + 3 * len(DILATED_PATTERNS))],
        compiler_params=pltpu.CompilerParams(dimension_semantics=("arbitrary", "arbitrary"),
                                             vmem_limit_bytes=VMEM_LIMIT),
        name="dilated",
    )(aq, ak, av)


def _out_ffn_kernel(ret_ref, att_ref, x_ref, wo_ref, g1_ref, g2_ref, g3_ref, wg_ref, wu_ref, wd_ref,
                    o_ref):
    n_sub = x_ref.shape[0] // FFN_SUB_ROWS

    def head(s):
        rows = pl.ds(s * FFN_SUB_ROWS, FFN_SUB_ROWS)
        mix = (jnp.dot(ret_ref[rows, :], wo_ref[:RET_V_W, :], preferred_element_type=F32)
               + jnp.dot(att_ref[rows, :].astype(BF16), wo_ref[RET_V_W:, :], preferred_element_type=F32))
        x1 = x_ref[rows, :] + _rms(mix, g1_ref[...])
        return x1, _rms(x1, g2_ref[...]).astype(BF16)

    def swiglu(h):
        acc = None
        for c in range(N_FF_CHUNKS):
            cols = slice(c * FF_CHUNK, (c + 1) * FF_CHUNK)
            a = jnp.dot(h, wg_ref[:, cols], preferred_element_type=F32)
            u = jnp.dot(h, wu_ref[:, cols], preferred_element_type=F32)
            act = (a * jax.nn.sigmoid(a) * u).astype(BF16)
            d = jnp.dot(act, wd_ref[cols, :], preferred_element_type=F32)
            acc = d if acc is None else acc + d
        return acc

    cur = head(0)
    for s in range(n_sub):
        nxt = head(s + 1) if s + 1 < n_sub else None
        x1, h = cur
        o_ref[pl.ds(s * FFN_SUB_ROWS, FFN_SUB_ROWS), :] = x1 + _rms(swiglu(h), g3_ref[...])
        cur = nxt


def _out_ffn(ret, att, x2, wo, g1, g2, g3, wg, wu, wd, tm):
    t = x2.shape[0]
    row = lambda w: pl.BlockSpec((tm, w), lambda i: (i, 0))
    consts = (wo, g1, g2, g3, wg, wu, wd)
    return pl.pallas_call(
        _out_ffn_kernel,
        out_shape=jax.ShapeDtypeStruct((t, D_MODEL), F32),
        grid=(t // tm,),
        in_specs=[row(RET_V_W), row(ATT_W), row(D_MODEL)] + [_const_spec(c.shape) for c in consts],
        out_specs=row(D_MODEL),
        compiler_params=pltpu.CompilerParams(dimension_semantics=("arbitrary",),
                                             vmem_limit_bytes=VMEM_LIMIT),
        name="out_ffn",
    )(ret, att, x2, *consts)


def _layer(x2, pos2, batch, seq, w_in, w_out, g_pre_mix, g_post_mix, g_pre_ffn, g_post_ffn,
           w_gate, w_up, w_down):
    row = lambda g: g.reshape(1, D_MODEL).astype(F32)
    rq, rqd, rkt, rkdt, rv, rg, aq, ak, av = _in_proj(x2, pos2, row(g_pre_mix), w_in.astype(BF16), IN_PROJ_ROWS)
    ret = _retention(rq, rqd, rkt, rkdt, rv, rg, batch, seq)
    att = _dilated(aq, ak, av, batch, seq)
    return _out_ffn(ret, att, x2, w_out.astype(BF16), row(g_post_mix), row(g_pre_ffn), row(g_post_ffn),
                    w_gate.astype(BF16), w_up.astype(BF16), w_down.astype(BF16), FFN_ROWS)


def kernel(x, positions, w_in, w_out, g_pre_mix, g_post_mix, g_pre_ffn, g_post_ffn, w_gate, w_up, w_down):
    batch, seq, d = x.shape
    assert d == D_MODEL and seq % (ATT_BLOCK * DILATED_PATTERNS[-1][1]) == 0
    assert (batch * seq) % FFN_ROWS == 0 and (batch * seq) % IN_PROJ_ROWS == 0
    x2 = x.reshape(batch * seq, d)
    pos2 = positions.reshape(batch * seq, 1)
    for l in range(w_in.shape[0]):
        x2 = _layer(x2, pos2, batch, seq, w_in[l], w_out[l], g_pre_mix[l], g_post_mix[l], g_pre_ffn[l],
                    g_post_ffn[l], w_gate[l], w_up[l], w_down[l])
    return x2.reshape(batch, seq, d)
```

```python
import functools

import numpy as np
import jax
import jax.numpy as jnp
from jax import lax
from jax.experimental import pallas as pl
from jax.experimental.pallas import tpu as pltpu

F32 = jnp.float32
BF16 = jnp.bfloat16

D_MODEL = 1024
RET_HEADS = 8
RET_QK = 32
RET_V = 64
RET_QK_W = RET_HEADS * RET_QK
RET_V_W = RET_HEADS * RET_V
RET_CHUNK = 128
RET_ROT_BASE = 10000.0
ATT_HEADS = 8
ATT_DIM = 64
ATT_W = ATT_HEADS * ATT_DIM
ATT_BLOCK = 128
DILATED_PATTERNS = ((128, 1), (512, 4), (2048, 16))
ROPE_THETA = 500000.0
ROPE_DIM = ATT_DIM // 4
D_FF = 2816
RMS_EPS = 1e-6
GN_EPS = 1e-5
PROJ_WIDTH = 2 * RET_QK_W + 2 * RET_V_W + 3 * ATT_W

LANES = 128
BF16_SUBLANES = 16
FF_CHUNK = 256
N_FF_CHUNKS = D_FF // FF_CHUNK
IN_PROJ_ROWS = 1024
IN_PROJ_SUB_ROWS = 512
FFN_ROWS = 1024
FFN_SUB_ROWS = 512
VMEM_LIMIT = 56 * 1024 * 1024
CAST_COLS = 256
LOG2E = 1.4426950408889634
RET_GROUP = 8
STAGE_DIL = 4

TRIG_LANES = 32
TRIG_PACK = LANES // TRIG_LANES
TRIG_ONE_LANE = 24


def _rms(x, g):
    return x * lax.rsqrt(jnp.mean(x * x, axis=-1, keepdims=True) + RMS_EPS) * g


def _rot_tables():
    ret_f = 1.0 / (RET_ROT_BASE ** jnp.linspace(0.0, 1.0, RET_QK // 2, dtype=F32))
    att_f = ROPE_THETA ** (-jnp.arange(0, ROPE_DIM, 2, dtype=F32) / ROPE_DIM)
    tok = jnp.concatenate([ret_f, att_f, jnp.zeros((TRIG_LANES - 24,), F32)])
    freq = jnp.stack([jnp.zeros((LANES,), F32).at[j * TRIG_LANES:(j + 1) * TRIG_LANES].set(tok)
                      for j in range(TRIG_PACK)])
    e = np.zeros((TRIG_PACK, 2 * LANES, 4 * LANES), np.float32)
    for j in range(TRIG_PACK):
        cos0, sin0 = j * TRIG_LANES, LANES + j * TRIG_LANES
        for c in range(LANES):
            d = c % RET_QK
            e[j, cos0 + d % 16, c] = 1.0
            e[j, sin0 + d % 16, LANES + c] = -1.0 if d < 16 else 1.0
            d = c % ATT_DIM
            if d < ROPE_DIM:
                e[j, cos0 + 16 + d % 8, 2 * LANES + c] = 1.0
                e[j, sin0 + 16 + d % 8, 3 * LANES + c] = -1.0 if d < 8 else 1.0
            else:
                e[j, cos0 + TRIG_ONE_LANE, 2 * LANES + c] = 1.0
    return freq, jnp.asarray(e, BF16)


def _rotate(x, cos, sin, half, group):
    lane = lax.broadcasted_iota(jnp.int32, x.shape, 1)
    up = pltpu.roll(x, LANES - half, 1)
    dn = pltpu.roll(x, half, 1)
    partner = jnp.where((lane % group) < half, up, dn)
    return x * cos + partner * sin


def _in_proj_kernel(x_ref, pos_ref, g_ref, w_ref, freq_ref, e_ref, qdec_ref, kdec_ref,
                    rq_ref, rqd_ref, rkt_ref, rkdt_ref, rv_ref, rg_ref, aq_ref, ak_ref, av_ref, wbf_ref):
    @pl.when(pl.program_id(0) == 0)
    def _():
        for c in range(0, PROJ_WIDTH, CAST_COLS):
            wbf_ref[:, c:c + CAST_COLS] = w_ref[:, c:c + CAST_COLS].astype(BF16)

    n_sub = x_ref.shape[0] // IN_PROJ_SUB_ROWS
    col = {}
    c0 = 0
    for name, width in (("rq", RET_QK_W), ("rk", RET_QK_W), ("rv", RET_V_W), ("rg", RET_V_W),
                        ("aq", ATT_W), ("ak", ATT_W), ("av", ATT_W)):
        col[name] = (c0, width)
        c0 += width

    def tables(s):
        blk = IN_PROJ_SUB_ROWS // TRIG_PACK
        ang = None
        for j in range(TRIG_PACK):
            pj = pos_ref[pl.ds(s * IN_PROJ_SUB_ROWS + j * blk, blk), :].astype(F32)
            term = pj * freq_ref[j:j + 1, :]
            ang = term if ang is None else ang + term
        cs = jnp.concatenate([jnp.cos(ang), jnp.sin(ang)], axis=1)
        hi = cs.astype(BF16)
        lo = (cs - hi.astype(F32)).astype(BF16)
        return jnp.concatenate(
            [jnp.dot(hi, e_ref[j], preferred_element_type=F32) + jnp.dot(lo, e_ref[j], preferred_element_type=F32)
             for j in range(TRIG_PACK)], axis=0)

    def body(s):
        rows = pl.ds(s * IN_PROJ_SUB_ROWS, IN_PROJ_SUB_ROWS)
        h = _rms(x_ref[rows, :], g_ref[...]).astype(BF16)

        def proj(name):
            c, width = col[name]
            return jnp.dot(h, wbf_ref[:, c:c + width], preferred_element_type=F32)

        rv_ref[rows, :] = proj("rv").astype(BF16)
        rg_ref[rows, :] = proj("rg").astype(BF16)
        av_ref[rows, :] = proj("av")
        tab = tables(s)
        cos_r, sin_r = tab[:, 0:LANES], tab[:, LANES:2 * LANES]
        cos_a, sin_a = tab[:, 2 * LANES:3 * LANES], tab[:, 3 * LANES:]
        for name, ref, cos, sin, half, group, scale, dec_ref, decayed_ref, transposed in (
                ("rq", rq_ref, cos_r, sin_r, RET_QK // 2, RET_QK, None, qdec_ref, rqd_ref, False),
                ("rk", rkt_ref, cos_r, sin_r, RET_QK // 2, RET_QK, RET_QK ** -0.5, kdec_ref, rkdt_ref, True),
                ("aq", aq_ref, cos_a, sin_a, ROPE_DIM // 2, ATT_DIM, ATT_DIM ** -0.5 * LOG2E, None, None, False),
                ("ak", ak_ref, cos_a, sin_a, ROPE_DIM // 2, ATT_DIM, None, None, None, False)):
            p = proj(name)
            for j in range(p.shape[1] // LANES):
                lanes = slice(j * LANES, (j + 1) * LANES)
                r = _rotate(p[:, lanes], cos, sin, half, group)
                if scale is not None:
                    r = r * scale
                outs = [(ref, r)] + ([(decayed_ref, r * dec_ref[:, lanes])] if dec_ref is not None else [])
                for dst, val in outs:
                    if transposed:
                        dst[lanes, rows] = val.T.astype(dst.dtype)
                    else:
                        dst[rows, lanes] = val.astype(dst.dtype)

    for s in range(n_sub):
        body(s)


def _const_spec(shape):
    return pl.BlockSpec(shape, lambda *_: (0,) * len(shape), pipeline_mode=pl.Buffered(1))


def _in_proj(x2, pos2, g, w, tm):
    t = x2.shape[0]
    freq, e = _rot_tables()
    tables = _retention_tables()
    assert IN_PROJ_SUB_ROWS % RET_CHUNK == 0
    reps = IN_PROJ_SUB_ROWS // RET_CHUNK
    qdec, kdec = (jnp.asarray(np.tile(tables[n], (reps, 1))) for n in ("q_decay", "k_decay"))
    consts = (g, w, freq, e, qdec, kdec)
    row = lambda w: pl.BlockSpec((tm, w), lambda i: (i, 0))
    outs = [((t, RET_QK_W), BF16, row(RET_QK_W))] * 2
    outs += [((RET_QK_W, t), BF16, pl.BlockSpec((RET_QK_W, tm), lambda i: (0, i)))] * 2
    outs += [((t, RET_V_W), BF16, row(RET_V_W))] * 2 + [((t, ATT_W), F32, row(ATT_W))] * 3
    return pl.pallas_call(
        _in_proj_kernel,
        out_shape=tuple(jax.ShapeDtypeStruct(shape, dt) for shape, dt, _ in outs),
        grid=(t // tm,),
        in_specs=[row(D_MODEL), row(1)] + [_const_spec(c.shape) for c in consts],
        out_specs=tuple(spec for _, _, spec in outs),
        scratch_shapes=[pltpu.VMEM(w.shape, BF16)],
        compiler_params=pltpu.CompilerParams(dimension_semantics=("arbitrary",),
                                             vmem_limit_bytes=VMEM_LIMIT),
        name="in_proj",
    )(x2, pos2, *consts)


def _retention_tables():
    h, c = RET_HEADS, RET_CHUNK
    f = np.float32
    log_g = np.log1p(-np.exp2(f(-5.0) - np.arange(h, dtype=f))).astype(f)
    idx = np.arange(c, dtype=f)
    diff = idx[:, None] - idx[None, :]
    inner = np.where(diff >= 0, np.exp(log_g[:, None, None] * np.maximum(diff, f(0.0))), f(0.0)).astype(f)
    q_decay = np.exp(log_g[:, None] * (idx + f(1.0))[None, :]).astype(f)
    k_decay = np.exp(log_g[:, None] * (f(c) - f(1.0) - idx)[None, :]).astype(f)
    chunk_decay = np.exp(log_g * f(c)).astype(f)
    rows = np.arange(RET_QK_W)[:, None] // RET_QK
    cols = np.arange(RET_V_W)[None, :] // RET_V
    gi = np.arange(2 * LANES) // RET_V
    return {
        "inner_decay": np.concatenate([inner[0::2], inner[1::2]], axis=-1),
        "q_decay": np.repeat(q_decay.T, RET_QK, axis=1),
        "k_decay": np.repeat(k_decay.T, RET_QK, axis=1),
        "chunk_decay": np.repeat(chunk_decay, RET_V)[None, :],
        "state_mask": (rows == cols).astype(f),
        "group_mean": (gi[:, None] == gi[None, :]).astype(f) / f(RET_V),
    }


def _retention_kernel(q_ref, qd_ref, kt_ref, kdt_ref, v_ref, g_ref, dec_ref, cd_ref, bd_ref, gn_ref, wo_ref,
                      o_ref, wo_bf_ref):
    wo_bf_ref[...] = wo_ref[...].astype(BF16)
    n_chunks = q_ref.shape[0] // RET_CHUNK
    lane_v = lax.broadcasted_iota(jnp.int32, (RET_CHUNK, LANES), 1)
    pairs = range(RET_HEADS // 2)

    def rows(ref, c):
        return ref[c * RET_CHUNK:(c + 1) * RET_CHUNK, :]

    def two_heads_kt(kt, j):
        def only(lo):
            pieces = [(lo, None), (RET_QK, kt[lo:lo + RET_QK, :]), (RET_QK_W - lo - RET_QK, None)]
            return jnp.concatenate([jnp.zeros((n, RET_CHUNK), BF16) if p is None else p
                                    for n, p in pieces if n > 0], axis=0)
        return jnp.concatenate([only(2 * j * RET_QK), only((2 * j + 1) * RET_QK)], axis=1)

    def two_heads_v(v, j):
        vj = v[:, j * LANES:(j + 1) * LANES]
        zero = jnp.zeros_like(vj)
        return jnp.concatenate([jnp.where(lane_v < RET_V, vj, zero),
                                jnp.where(lane_v >= RET_V, vj, zero)], axis=0)

    def stage_scores(chunks):
        return [[jnp.dot(rows(q_ref, c), two_heads_kt(kt_ref[:, c * RET_CHUNK:(c + 1) * RET_CHUNK], j),
                         preferred_element_type=F32) for j in pairs] for c in chunks]

    def stage_kv(chunks):
        return [jnp.dot(kdt_ref[:, c * RET_CHUNK:(c + 1) * RET_CHUNK], rows(v_ref, c),
                        preferred_element_type=F32) for c in chunks]

    def stage_pv(chunks, scores):
        return [jnp.concatenate(
            [jnp.dot((sc[j] * dec_ref[j]).astype(BF16), two_heads_v(rows(v_ref, c), j),
                     preferred_element_type=F32) for j in pairs], axis=1)
            for c, sc in zip(chunks, scores)]

    def stage_recurrence(chunks, kvs, inners, state):
        outs = []
        for c, kv, inner in zip(chunks, kvs, inners):
            outs.append(inner + jnp.dot(rows(qd_ref, c), state.astype(BF16), preferred_element_type=F32))
            state = state * cd_ref[...] + kv * bd_ref[...]
        return outs, state

    def halves(x):
        return [x[:, t * 2 * LANES:(t + 1) * 2 * LANES] for t in range(2)]

    def stage_mean(outs):
        return [[o_t - jnp.dot(o_t.astype(BF16), gn_ref[...], preferred_element_type=F32) for o_t in halves(o)]
                for o in outs]

    def stage_var(ds):
        return [[jnp.dot((d * d).astype(BF16), gn_ref[...], preferred_element_type=F32) for d in dv] for dv in ds]

    def stage_store(chunks, ds, vars_):
        for c, dv, vv in zip(chunks, ds, vars_):
            y = jnp.concatenate([d * lax.rsqrt(var + GN_EPS) for d, var in zip(dv, vv)], axis=1)
            g = rows(g_ref, c).astype(F32)
            o_ref[c * RET_CHUNK:(c + 1) * RET_CHUNK, :] = (y * (g * jax.nn.sigmoid(g))).astype(BF16)

    assert n_chunks % RET_GROUP == 0
    groups = [list(range(g0, g0 + RET_GROUP)) for g0 in range(0, n_chunks, RET_GROUP)]
    state = jnp.zeros((RET_QK_W, RET_V_W), F32)
    pending = None
    sc, kv = stage_scores(groups[0]), stage_kv(groups[0])
    for gi, chunks in enumerate(groups):
        inner = stage_pv(chunks, sc)
        if gi + 1 < len(groups):
            nxt = (stage_scores(groups[gi + 1]), stage_kv(groups[gi + 1]))
        outs, state = stage_recurrence(chunks, kv, inner, state)
        if pending is not None:
            stage_store(pending[0], pending[1], stage_var(pending[1]))
        pending = (chunks, stage_mean(outs))
        if gi + 1 < len(groups):
            sc, kv = nxt
    stage_store(pending[0], pending[1], stage_var(pending[1]))


def _retention(rq, rqd, rkt, rkdt, rv, rg, w_out, batch, seq):
    tables = _retention_tables()
    consts = tuple(jnp.asarray(tables[n]) for n in ("inner_decay", "chunk_decay", "state_mask"))
    consts += (jnp.asarray(tables["group_mean"], BF16),)
    seq_spec = lambda w: pl.BlockSpec((seq, w), lambda b: (b, 0))
    t_spec = pl.BlockSpec((RET_QK_W, seq), lambda b: (0, b))
    wo_rows = w_out.shape[0] // batch
    assert wo_rows * batch == w_out.shape[0] and wo_rows % BF16_SUBLANES == 0
    wo_spec = pl.BlockSpec((wo_rows, w_out.shape[1]), lambda b: (b, 0))
    return pl.pallas_call(
        _retention_kernel,
        out_shape=(jax.ShapeDtypeStruct((batch * seq, RET_V_W), BF16), jax.ShapeDtypeStruct(w_out.shape, BF16)),
        grid=(batch,),
        in_specs=[seq_spec(RET_QK_W)] * 2 + [t_spec] * 2 + [seq_spec(RET_V_W)] * 2
                 + [_const_spec(c.shape) for c in consts] + [wo_spec],
        out_specs=(seq_spec(RET_V_W), wo_spec),
        compiler_params=pltpu.CompilerParams(dimension_semantics=("arbitrary",),
                                             vmem_limit_bytes=VMEM_LIMIT),
        name="retention",
    )(rq, rqd, rkt, rkdt, rv, rg, *consts, w_out)


def _attend(qb, kb, vb, bias):
    nk = kb.shape[0]
    lane = lax.broadcasted_iota(jnp.int32, (ATT_BLOCK, LANES), 1)
    first_head = lane < ATT_DIM
    kbf = kb.astype(BF16)
    vext = jnp.concatenate([vb.astype(BF16), jnp.ones((nk, LANES), BF16)], axis=1)
    res = []
    for head_lanes in (first_head, jnp.logical_not(first_head)):
        qm = jnp.where(head_lanes, qb, 0.0).astype(BF16)
        s = lax.dot_general(qm, kbf, (((1,), (1,)), ((), ())), preferred_element_type=F32) + bias
        m = jnp.max(s, axis=-1, keepdims=True)
        p = jnp.exp2(s - m)
        pv = jnp.dot(p.astype(BF16), vext, preferred_element_type=F32)
        res.append((pv[:, :LANES], pv[:, LANES:], m))
    return tuple(jnp.where(first_head, a, b) for a, b in zip(*res))


def _attend_first(qb, kb, vb, bias2):
    lane = lax.broadcasted_iota(jnp.int32, (ATT_BLOCK, LANES), 1)
    first_head = lane < ATT_DIM
    sel_a = lambda x: jnp.where(first_head, x, 0.0).astype(BF16)
    sel_b = lambda x: jnp.where(first_head, 0.0, x).astype(BF16)
    one = jnp.ones((ATT_BLOCK, LANES), F32)
    kcat = jnp.concatenate([sel_a(kb), sel_b(kb)], axis=0)
    s = lax.dot_general(qb.astype(BF16), kcat, (((1,), (1,)), ((), ())), preferred_element_type=F32) + bias2
    ma = jnp.max(s[:, :ATT_BLOCK], axis=-1, keepdims=True)
    mb = jnp.max(s[:, ATT_BLOCK:], axis=-1, keepdims=True)
    p = jnp.concatenate([jnp.exp2(s[:, :ATT_BLOCK] - ma), jnp.exp2(s[:, ATT_BLOCK:] - mb)], axis=1)
    vcat = jnp.concatenate([jnp.concatenate([sel_a(vb), sel_a(one)], axis=1),
                            jnp.concatenate([sel_b(vb), sel_b(one)], axis=1)], axis=0)
    pv = jnp.dot(p.astype(BF16), vcat, preferred_element_type=F32)
    return pv[:, :LANES], pv[:, LANES:], jnp.where(first_head, ma, mb)


def _dilated_kernel(q_ref, k_ref, v_ref, wg_ref, wu_ref, wd_ref, o_ref, wg_bf_ref, wu_bf_ref, wd_bf_ref,
                    bias_ref, bias2_ref, q4_ref, k4_ref, v4_ref, *scratch):
    @pl.when(pl.program_id(1) % 2 == 0)
    def _():
        for src, dst in ((wg_ref, wg_bf_ref), (wu_ref, wu_bf_ref), (wd_ref, wd_bf_ref)):
            dst[...] = src[...].astype(BF16)

    seq = q_ref.shape[0]
    blk = ATT_BLOCK
    sd = STAGE_DIL
    lc = seq // sd
    a = lax.broadcasted_iota(jnp.int32, (blk, 2 * blk), 0)
    kk = lax.broadcasted_iota(jnp.int32, (blk, 2 * blk), 1)
    dist = blk + a - kk
    bias_ref[...] = jnp.where((dist >= 0) & (dist <= blk), 0.0, -jnp.inf)
    causal = jnp.where(kk % blk <= a, 0.0, -jnp.inf)
    bias2_ref[...] = causal
    for r in range(sd):
        for src, dst in ((q_ref, q4_ref), (k_ref, k4_ref), (v_ref, v4_ref)):
            dst[r * lc:(r + 1) * lc, :] = src[pl.ds(r, lc, stride=sd), :]

    def view(p_idx, cls, b, n):
        dil = DILATED_PATTERNS[p_idx][1]
        if dil == 1:
            return (q_ref, k_ref, v_ref), pl.ds(b * blk, n)
        if dil == sd:
            return (q4_ref, k4_ref, v4_ref), pl.ds(cls * lc + b * blk, n)
        return (q4_ref, k4_ref, v4_ref), pl.ds((cls % sd) * lc + cls // sd + b * blk * sd, n, stride=sd)

    def run(tasks, first):
        loaded = []
        for p_idx, cls, b in tasks:
            (qr, kr, vr), qs = view(p_idx, cls, b, blk)
            _, ks = (None, qs) if first else view(p_idx, cls, b - 1, 2 * blk)
            loaded.append((qr[qs, :], kr[ks, :], vr[ks, :], qs))
        if first:
            bias2 = bias2_ref[...]
            results = [_attend_first(qb, kb, vb, bias2) for qb, kb, vb, _ in loaded]
        else:
            bias = bias_ref[...]
            results = [_attend(qb, kb, vb, bias) for qb, kb, vb, _ in loaded]
        for (p_idx, _, _), (_, _, _, dst), res in zip(tasks, loaded, results):
            for ref, val in zip(scratch[3 * p_idx:3 * p_idx + 3], res):
                ref[dst, :] = val

    n_blocks = [seq // dil // blk for _, dil in DILATED_PATTERNS]
    run([(p, cls, 0) for p, (_, dil) in enumerate(DILATED_PATTERNS) for cls in range(dil)], True)
    for p, (_, dil) in enumerate(DILATED_PATTERNS):
        if n_blocks[p] > 1:
            run([(p, cls, b) for b in range(1, n_blocks[p]) for cls in range(dil)], False)

    def combine(c, carry):
        for r in range(sd):
            nat = pl.ds(r + c * blk * sd, blk, stride=sd)
            cls = pl.ds(pl.multiple_of(r * lc + c * blk, blk), blk)
            sls = [nat if dil == 1 else cls for _, dil in DILATED_PATTERNS]
            ms = [scratch[3 * p + 2][sl, :] for p, sl in enumerate(sls)]
            m = functools.reduce(jnp.maximum, ms)
            ws = [jnp.exp2(mp - m) for mp in ms]
            num = functools.reduce(lambda x, y: x + y,
                                   [w * scratch[3 * p][sl, :] for p, (w, sl) in enumerate(zip(ws, sls))])
            den = functools.reduce(lambda x, y: x + y,
                                   [w * scratch[3 * p + 1][sl, :] for p, (w, sl) in enumerate(zip(ws, sls))])
            o_ref[nat, :] = num / den
        return carry

    lax.fori_loop(0, lc // blk, combine, 0)


def _dilated(aq, ak, av, weights, batch, seq):
    dils = [dil for _, dil in DILATED_PATTERNS]
    assert dils == [1, STAGE_DIL, STAGE_DIL * STAGE_DIL]
    for window, dil in DILATED_PATTERNS:
        assert window // dil == ATT_BLOCK and (seq // dil) % ATT_BLOCK == 0
    spec = pl.BlockSpec((seq, LANES), lambda b, hp: (b, hp))
    n_pairs = ATT_W // LANES
    n_slices = batch * n_pairs // 2
    w_specs = []
    for w in weights:
        rows = w.shape[0] // n_slices
        assert rows * n_slices == w.shape[0] and rows % BF16_SUBLANES == 0 and n_pairs % 2 == 0
        w_specs.append(pl.BlockSpec((rows, w.shape[1]), lambda b, hp: (b * (n_pairs // 2) + hp // 2, 0)))
    return pl.pallas_call(
        _dilated_kernel,
        out_shape=(jax.ShapeDtypeStruct((batch * seq, ATT_W), F32),)
                  + tuple(jax.ShapeDtypeStruct(w.shape, BF16) for w in weights),
        grid=(batch, n_pairs),
        in_specs=[spec, spec, spec] + w_specs,
        out_specs=(spec,) + tuple(w_specs),
        scratch_shapes=[pltpu.VMEM((ATT_BLOCK, 2 * ATT_BLOCK), F32) for _ in range(2)]
                       + [pltpu.VMEM((seq, LANES), F32) for _ in range(3 + 3 * len(DILATED_PATTERNS))],
        compiler_params=pltpu.CompilerParams(dimension_semantics=("arbitrary", "arbitrary"),
                                             vmem_limit_bytes=VMEM_LIMIT),
        name="dilated",
    )(aq, ak, av, *weights)


def _out_ffn_kernel(ret_ref, att_ref, x_ref, wo_ref, g1_ref, g2_ref, g3_ref, wg_ref, wu_ref, wd_ref,
                    o_ref):
    n_sub = x_ref.shape[0] // FFN_SUB_ROWS

    def head(s):
        rows = pl.ds(s * FFN_SUB_ROWS, FFN_SUB_ROWS)
        mix = (jnp.dot(ret_ref[rows, :], wo_ref[:RET_V_W, :], preferred_element_type=F32)
               + jnp.dot(att_ref[rows, :].astype(BF16), wo_ref[RET_V_W:, :], preferred_element_type=F32))
        x1 = x_ref[rows, :] + _rms(mix, g1_ref[...])
        return x1, _rms(x1, g2_ref[...]).astype(BF16)

    def swiglu(h):
        acc = None
        for c in range(N_FF_CHUNKS):
            cols = slice(c * FF_CHUNK, (c + 1) * FF_CHUNK)
            a = jnp.dot(h, wg_ref[:, cols], preferred_element_type=F32)
            u = jnp.dot(h, wu_ref[:, cols], preferred_element_type=F32)
            act = (a * jax.nn.sigmoid(a) * u).astype(BF16)
            d = jnp.dot(act, wd_ref[cols, :], preferred_element_type=F32)
            acc = d if acc is None else acc + d
        return acc

    cur = head(0)
    for s in range(n_sub):
        nxt = head(s + 1) if s + 1 < n_sub else None
        x1, h = cur
        o_ref[pl.ds(s * FFN_SUB_ROWS, FFN_SUB_ROWS), :] = x1 + _rms(swiglu(h), g3_ref[...])
        cur = nxt


def _out_ffn(ret, att, x2, wo, g1, g2, g3, wg, wu, wd, tm):
    t = x2.shape[0]
    row = lambda w: pl.BlockSpec((tm, w), lambda i: (i, 0))
    consts = (wo, g1, g2, g3, wg, wu, wd)
    return pl.pallas_call(
        _out_ffn_kernel,
        out_shape=jax.ShapeDtypeStruct((t, D_MODEL), F32),
        grid=(t // tm,),
        in_specs=[row(RET_V_W), row(ATT_W), row(D_MODEL)] + [_const_spec(c.shape) for c in consts],
        out_specs=row(D_MODEL),
        compiler_params=pltpu.CompilerParams(dimension_semantics=("arbitrary",),
                                             vmem_limit_bytes=VMEM_LIMIT),
        name="out_ffn",
    )(ret, att, x2, *consts)


def _layer(x2, pos2, batch, seq, w_in, w_out, g_pre_mix, g_post_mix, g_pre_ffn, g_post_ffn,
           w_gate, w_up, w_down):
    row = lambda g: g.reshape(1, D_MODEL).astype(F32)
    rq, rqd, rkt, rkdt, rv, rg, aq, ak, av = _in_proj(x2, pos2, row(g_pre_mix), w_in, IN_PROJ_ROWS)
    ret, wo_bf = _retention(rq, rqd, rkt, rkdt, rv, rg, w_out, batch, seq)
    att, wg_bf, wu_bf, wd_bf = _dilated(aq, ak, av, (w_gate, w_up, w_down), batch, seq)
    return _out_ffn(ret, att, x2, wo_bf, row(g_post_mix), row(g_pre_ffn), row(g_post_ffn),
                    wg_bf, wu_bf, wd_bf, FFN_ROWS)


def kernel(x, positions, w_in, w_out, g_pre_mix, g_post_mix, g_pre_ffn, g_post_ffn, w_gate, w_up, w_down):
    batch, seq, d = x.shape
    assert d == D_MODEL and seq % (ATT_BLOCK * DILATED_PATTERNS[-1][1]) == 0
    assert (batch * seq) % FFN_ROWS == 0 and (batch * seq) % IN_PROJ_ROWS == 0
    x2 = x.reshape(batch * seq, d)
    pos2 = positions.reshape(batch * seq, 1)
    for l in range(w_in.shape[0]):
        x2 = _layer(x2, pos2, batch, seq, w_in[l], w_out[l], g_pre_mix[l], g_post_mix[l], g_pre_ffn[l],
                    g_post_ffn[l], w_gate[l], w_up[l], w_down[l])
    return x2.reshape(batch, seq, d)
```

```python
import functools

import numpy as np
import jax
import jax.numpy as jnp
from jax import lax
from jax.experimental import pallas as pl
from jax.experimental.pallas import tpu as pltpu

F32 = jnp.float32
BF16 = jnp.bfloat16

D_MODEL = 1024
RET_HEADS = 8
RET_QK = 32
RET_V = 64
RET_QK_W = RET_HEADS * RET_QK
RET_V_W = RET_HEADS * RET_V
RET_CHUNK = 128
RET_ROT_BASE = 10000.0
ATT_HEADS = 8
ATT_DIM = 64
ATT_W = ATT_HEADS * ATT_DIM
ATT_BLOCK = 128
DILATED_PATTERNS = ((128, 1), (512, 4), (2048, 16))
ROPE_THETA = 500000.0
ROPE_DIM = ATT_DIM // 4
D_FF = 2816
RMS_EPS = 1e-6
GN_EPS = 1e-5
PROJ_WIDTH = 2 * RET_QK_W + 2 * RET_V_W + 3 * ATT_W

LANES = 128
BF16_SUBLANES = 16
FF_CHUNK = 256
N_FF_CHUNKS = D_FF // FF_CHUNK
IN_PROJ_ROWS = 1024
IN_PROJ_SUB_ROWS = 512
FFN_ROWS = 1024
FFN_SUB_ROWS = 512
VMEM_LIMIT = 56 * 1024 * 1024
CAST_COLS = 256
LOG2E = 1.4426950408889634
RET_GROUP = 8
STAGE_DIL = 4

TRIG_LANES = 32
TRIG_PACK = LANES // TRIG_LANES
TRIG_ONE_LANE = 24


def _rms(x, g):
    return x * lax.rsqrt(jnp.mean(x * x, axis=-1, keepdims=True) + RMS_EPS) * g


def _rot_tables():
    ret_f = 1.0 / (RET_ROT_BASE ** jnp.linspace(0.0, 1.0, RET_QK // 2, dtype=F32))
    att_f = ROPE_THETA ** (-jnp.arange(0, ROPE_DIM, 2, dtype=F32) / ROPE_DIM)
    tok = jnp.concatenate([ret_f, att_f, jnp.zeros((TRIG_LANES - 24,), F32)])
    freq = jnp.stack([jnp.zeros((LANES,), F32).at[j * TRIG_LANES:(j + 1) * TRIG_LANES].set(tok)
                      for j in range(TRIG_PACK)])
    e = np.zeros((TRIG_PACK, 2 * LANES, 4 * LANES), np.float32)
    for j in range(TRIG_PACK):
        cos0, sin0 = j * TRIG_LANES, LANES + j * TRIG_LANES
        for c in range(LANES):
            d = c % RET_QK
            e[j, cos0 + d % 16, c] = 1.0
            e[j, sin0 + d % 16, LANES + c] = -1.0 if d < 16 else 1.0
            d = c % ATT_DIM
            if d < ROPE_DIM:
                e[j, cos0 + 16 + d % 8, 2 * LANES + c] = 1.0
                e[j, sin0 + 16 + d % 8, 3 * LANES + c] = -1.0 if d < 8 else 1.0
            else:
                e[j, cos0 + TRIG_ONE_LANE, 2 * LANES + c] = 1.0
    return freq, jnp.asarray(e, BF16)


def _rotate(x, cos, sin, half, group):
    lane = lax.broadcasted_iota(jnp.int32, x.shape, 1)
    up = pltpu.roll(x, LANES - half, 1)
    dn = pltpu.roll(x, half, 1)
    partner = jnp.where((lane % group) < half, up, dn)
    return x * cos + partner * sin


def _in_proj_kernel(x_ref, pos_ref, g_ref, w_ref, freq_ref, e_ref, qdec_ref, kdec_ref,
                    rq_ref, rqd_ref, rkt_ref, rkdt_ref, rv_ref, rg_ref, aq_ref, ak_ref, av_ref, wbf_ref):
    @pl.when(pl.program_id(0) == 0)
    def _():
        for c in range(0, PROJ_WIDTH, CAST_COLS):
            wbf_ref[:, c:c + CAST_COLS] = w_ref[:, c:c + CAST_COLS].astype(BF16)

    n_sub = x_ref.shape[0] // IN_PROJ_SUB_ROWS
    col = {}
    c0 = 0
    for name, width in (("rq", RET_QK_W), ("rk", RET_QK_W), ("rv", RET_V_W), ("rg", RET_V_W),
                        ("aq", ATT_W), ("ak", ATT_W), ("av", ATT_W)):
        col[name] = (c0, width)
        c0 += width

    def tables(s):
        assert IN_PROJ_SUB_ROWS == TRIG_PACK * LANES
        ang = None
        for j in range(TRIG_PACK):
            r = s * TRIG_PACK + j
            prow = pos_ref[r:r + 1, :].astype(F32)
            pj = jnp.broadcast_to(prow, (LANES, LANES)).T
            term = pj * freq_ref[j:j + 1, :]
            ang = term if ang is None else ang + term
        cs = jnp.concatenate([jnp.cos(ang), jnp.sin(ang)], axis=1)
        hi = cs.astype(BF16)
        lo = (cs - hi.astype(F32)).astype(BF16)
        return jnp.concatenate(
            [jnp.dot(hi, e_ref[j], preferred_element_type=F32) + jnp.dot(lo, e_ref[j], preferred_element_type=F32)
             for j in range(TRIG_PACK)], axis=0)

    def body(s):
        rows = pl.ds(s * IN_PROJ_SUB_ROWS, IN_PROJ_SUB_ROWS)
        h = _rms(x_ref[rows, :], g_ref[...]).astype(BF16)

        def proj(name):
            c, width = col[name]
            return jnp.dot(h, wbf_ref[:, c:c + width], preferred_element_type=F32)

        rv_ref[rows, :] = proj("rv").astype(BF16)
        tab = tables(s)
        cos_r, sin_r = tab[:, 0:LANES], tab[:, LANES:2 * LANES]
        cos_a, sin_a = tab[:, 2 * LANES:3 * LANES], tab[:, 3 * LANES:]
        for name, ref, cos, sin, half, group, scale, dec_ref, decayed_ref, transposed in (
                ("rq", rq_ref, cos_r, sin_r, RET_QK // 2, RET_QK, None, qdec_ref, rqd_ref, False),
                ("rk", rkt_ref, cos_r, sin_r, RET_QK // 2, RET_QK, RET_QK ** -0.5, kdec_ref, rkdt_ref, True),
                ("aq", aq_ref, cos_a, sin_a, ROPE_DIM // 2, ATT_DIM, ATT_DIM ** -0.5 * LOG2E, None, None, False),
                ("ak", ak_ref, cos_a, sin_a, ROPE_DIM // 2, ATT_DIM, None, None, None, False)):
            p = proj(name)
            for j in range(p.shape[1] // LANES):
                lanes = slice(j * LANES, (j + 1) * LANES)
                r = _rotate(p[:, lanes], cos, sin, half, group)
                if scale is not None:
                    r = r * scale
                outs = [(ref, r)] + ([(decayed_ref, r * dec_ref[:, lanes])] if dec_ref is not None else [])
                for dst, val in outs:
                    if transposed:
                        dst[lanes, rows] = val.T.astype(dst.dtype)
                    else:
                        dst[rows, lanes] = val.astype(dst.dtype)
        rg_ref[rows, :] = proj("rg").astype(BF16)
        av_ref[rows, :] = proj("av")

    for s in range(n_sub):
        body(s)


def _const_spec(shape):
    return pl.BlockSpec(shape, lambda *_: (0,) * len(shape), pipeline_mode=pl.Buffered(1))


def _in_proj(x2, pos2, g, w, tm):
    t = x2.shape[0]
    freq, e = _rot_tables()
    tables = _retention_tables()
    assert IN_PROJ_SUB_ROWS % RET_CHUNK == 0
    reps = IN_PROJ_SUB_ROWS // RET_CHUNK
    qdec, kdec = (jnp.asarray(np.tile(tables[n], (reps, 1))) for n in ("q_decay", "k_decay"))
    consts = (g, w, freq, e, qdec, kdec)
    row = lambda w: pl.BlockSpec((tm, w), lambda i: (i, 0))
    outs = [((t, RET_QK_W), BF16, row(RET_QK_W))] * 2
    outs += [((RET_QK_W, t), BF16, pl.BlockSpec((RET_QK_W, tm), lambda i: (0, i)))] * 2
    outs += [((t, RET_V_W), BF16, row(RET_V_W))] * 2 + [((t, ATT_W), F32, row(ATT_W))] * 3
    return pl.pallas_call(
        _in_proj_kernel,
        out_shape=tuple(jax.ShapeDtypeStruct(shape, dt) for shape, dt, _ in outs),
        grid=(t // tm,),
        in_specs=[row(D_MODEL), pl.BlockSpec((tm // LANES, LANES), lambda i: (i, 0))]
                 + [_const_spec(c.shape) for c in consts],
        out_specs=tuple(spec for _, _, spec in outs),
        scratch_shapes=[pltpu.VMEM(w.shape, BF16)],
        compiler_params=pltpu.CompilerParams(dimension_semantics=("arbitrary",),
                                             vmem_limit_bytes=VMEM_LIMIT),
        name="in_proj",
    )(x2, pos2, *consts)


def _retention_tables():
    h, c = RET_HEADS, RET_CHUNK
    f = np.float32
    log_g = np.log1p(-np.exp2(f(-5.0) - np.arange(h, dtype=f))).astype(f)
    idx = np.arange(c, dtype=f)
    diff = idx[:, None] - idx[None, :]
    inner = np.where(diff >= 0, np.exp(log_g[:, None, None] * np.maximum(diff, f(0.0))), f(0.0)).astype(f)
    q_decay = np.exp(log_g[:, None] * (idx + f(1.0))[None, :]).astype(f)
    k_decay = np.exp(log_g[:, None] * (f(c) - f(1.0) - idx)[None, :]).astype(f)
    chunk_decay = np.exp(log_g * f(c)).astype(f)
    rows = np.arange(2 * RET_QK)[:, None] // RET_QK
    cols = np.arange(2 * RET_V)[None, :] // RET_V
    gi = np.arange(2 * LANES) // RET_V
    return {
        "inner_decay": np.concatenate([inner[0::2], inner[1::2]], axis=-1),
        "q_decay": np.repeat(q_decay.T, RET_QK, axis=1),
        "k_decay": np.repeat(k_decay.T, RET_QK, axis=1),
        "chunk_decay": np.repeat(chunk_decay, RET_V)[None, :],
        "state_mask": (rows == cols).astype(f),
        "group_mean": (gi[:, None] == gi[None, :]).astype(f) / f(RET_V),
    }


def _retention_kernel(q_ref, qd_ref, kt_ref, kdt_ref, v_ref, g_ref, dec_ref, cd_ref, bd_ref, gn_ref, wo_ref,
                      o_ref, wo_bf_ref):
    wo_bf_ref[...] = wo_ref[...].astype(BF16)
    n_chunks = q_ref.shape[0] // RET_CHUNK
    lane_v = lax.broadcasted_iota(jnp.int32, (RET_CHUNK, LANES), 1)
    pairs = range(RET_HEADS // 2)

    def rows(ref, c):
        return ref[c * RET_CHUNK:(c + 1) * RET_CHUNK, :]

    def two_heads_kt(kt, j):
        def only(lo):
            pieces = [(lo, None), (RET_QK, kt[lo:lo + RET_QK, :]), (RET_QK_W - lo - RET_QK, None)]
            return jnp.concatenate([jnp.zeros((n, RET_CHUNK), BF16) if p is None else p
                                    for n, p in pieces if n > 0], axis=0)
        return jnp.concatenate([only(2 * j * RET_QK), only((2 * j + 1) * RET_QK)], axis=1)

    def two_heads_v(v, j):
        vj = v[:, j * LANES:(j + 1) * LANES]
        zero = jnp.zeros_like(vj)
        return jnp.concatenate([jnp.where(lane_v < RET_V, vj, zero),
                                jnp.where(lane_v >= RET_V, vj, zero)], axis=0)

    def stage_scores(chunks):
        return [[jnp.dot(rows(q_ref, c), two_heads_kt(kt_ref[:, c * RET_CHUNK:(c + 1) * RET_CHUNK], j),
                         preferred_element_type=F32) for j in pairs] for c in chunks]

    pair_rows = 2 * RET_QK

    def stage_kv(chunks):
        return [[jnp.dot(kdt_ref[j * pair_rows:(j + 1) * pair_rows, c * RET_CHUNK:(c + 1) * RET_CHUNK],
                         rows(v_ref, c)[:, j * LANES:(j + 1) * LANES], preferred_element_type=F32)
                 for j in pairs] for c in chunks]

    def stage_pv(chunks, scores):
        return [jnp.concatenate(
            [jnp.dot((sc[j] * dec_ref[j]).astype(BF16), two_heads_v(rows(v_ref, c), j),
                     preferred_element_type=F32) for j in pairs], axis=1)
            for c, sc in zip(chunks, scores)]

    def block_diagonal(tiles):
        zero = jnp.zeros((pair_rows, LANES), BF16)
        return jnp.concatenate(
            [jnp.concatenate([t.astype(BF16) if i == j else zero for i in pairs], axis=1)
             for j, t in enumerate(tiles)], axis=0)

    def stage_recurrence(chunks, kvs, inners, state):
        outs = []
        for c, kv, inner in zip(chunks, kvs, inners):
            outs.append(inner + jnp.dot(rows(qd_ref, c), block_diagonal(state), preferred_element_type=F32))
            state = [s * cd_ref[:, j * LANES:(j + 1) * LANES] + kv[j] * bd_ref[...] for j, s in enumerate(state)]
        return outs, state

    def halves(x):
        return [x[:, t * 2 * LANES:(t + 1) * 2 * LANES] for t in range(2)]

    def stage_mean(outs):
        return [[o_t - jnp.dot(o_t.astype(BF16), gn_ref[...], preferred_element_type=F32) for o_t in halves(o)]
                for o in outs]

    def stage_var(ds):
        return [[jnp.dot((d * d).astype(BF16), gn_ref[...], preferred_element_type=F32) for d in dv] for dv in ds]

    def stage_store(chunks, ds, vars_):
        for c, dv, vv in zip(chunks, ds, vars_):
            y = jnp.concatenate([d * lax.rsqrt(var + GN_EPS) for d, var in zip(dv, vv)], axis=1)
            g = rows(g_ref, c).astype(F32)
            o_ref[c * RET_CHUNK:(c + 1) * RET_CHUNK, :] = (y * (g * jax.nn.sigmoid(g))).astype(BF16)

    assert n_chunks % RET_GROUP == 0
    groups = [list(range(g0, g0 + RET_GROUP)) for g0 in range(0, n_chunks, RET_GROUP)]
    state = [jnp.zeros((pair_rows, LANES), F32) for _ in pairs]
    pending = None
    sc, kv = stage_scores(groups[0]), stage_kv(groups[0])
    for gi, chunks in enumerate(groups):
        inner = stage_pv(chunks, sc)
        if gi + 1 < len(groups):
            nxt = (stage_scores(groups[gi + 1]), stage_kv(groups[gi + 1]))
        outs, state = stage_recurrence(chunks, kv, inner, state)
        if pending is not None:
            stage_store(pending[0], pending[1], stage_var(pending[1]))
        pending = (chunks, stage_mean(outs))
        if gi + 1 < len(groups):
            sc, kv = nxt
    stage_store(pending[0], pending[1], stage_var(pending[1]))


def _retention(rq, rqd, rkt, rkdt, rv, rg, w_out, batch, seq):
    tables = _retention_tables()
    consts = tuple(jnp.asarray(tables[n]) for n in ("inner_decay", "chunk_decay", "state_mask"))
    consts += (jnp.asarray(tables["group_mean"], BF16),)
    seq_spec = lambda w: pl.BlockSpec((seq, w), lambda b: (b, 0))
    t_spec = pl.BlockSpec((RET_QK_W, seq), lambda b: (0, b))
    wo_rows = w_out.shape[0] // batch
    assert wo_rows * batch == w_out.shape[0] and wo_rows % BF16_SUBLANES == 0
    wo_spec = pl.BlockSpec((wo_rows, w_out.shape[1]), lambda b: (b, 0))
    return pl.pallas_call(
        _retention_kernel,
        out_shape=(jax.ShapeDtypeStruct((batch * seq, RET_V_W), BF16), jax.ShapeDtypeStruct(w_out.shape, BF16)),
        grid=(batch,),
        in_specs=[seq_spec(RET_QK_W)] * 2 + [t_spec] * 2 + [seq_spec(RET_V_W)] * 2
                 + [_const_spec(c.shape) for c in consts] + [wo_spec],
        out_specs=(seq_spec(RET_V_W), wo_spec),
        compiler_params=pltpu.CompilerParams(dimension_semantics=("arbitrary",),
                                             vmem_limit_bytes=VMEM_LIMIT),
        name="retention",
    )(rq, rqd, rkt, rkdt, rv, rg, *consts, w_out)


def _attend(qb, kb, vb, bias):
    nk = kb.shape[0]
    lane = lax.broadcasted_iota(jnp.int32, (ATT_BLOCK, LANES), 1)
    first_head = lane < ATT_DIM
    kbf = kb.astype(BF16)
    vext = jnp.concatenate([vb.astype(BF16), jnp.ones((nk, LANES), BF16)], axis=1)
    res = []
    for head_lanes in (first_head, jnp.logical_not(first_head)):
        qm = jnp.where(head_lanes, qb, 0.0).astype(BF16)
        s = lax.dot_general(qm, kbf, (((1,), (1,)), ((), ())), preferred_element_type=F32) + bias
        m = jnp.max(s, axis=-1, keepdims=True)
        p = jnp.exp2(s - m)
        pv = jnp.dot(p.astype(BF16), vext, preferred_element_type=F32)
        res.append((pv[:, :LANES], pv[:, LANES:], m))
    return tuple(jnp.where(first_head, a, b) for a, b in zip(*res))


def _attend_first(qb, kb, vb, bias2):
    lane = lax.broadcasted_iota(jnp.int32, (ATT_BLOCK, LANES), 1)
    first_head = lane < ATT_DIM
    sel_a = lambda x: jnp.where(first_head, x, 0.0).astype(BF16)
    sel_b = lambda x: jnp.where(first_head, 0.0, x).astype(BF16)
    one = jnp.ones((ATT_BLOCK, LANES), F32)
    kcat = jnp.concatenate([sel_a(kb), sel_b(kb)], axis=0)
    s = lax.dot_general(qb.astype(BF16), kcat, (((1,), (1,)), ((), ())), preferred_element_type=F32) + bias2
    ma = jnp.max(s[:, :ATT_BLOCK], axis=-1, keepdims=True)
    mb = jnp.max(s[:, ATT_BLOCK:], axis=-1, keepdims=True)
    p = jnp.concatenate([jnp.exp2(s[:, :ATT_BLOCK] - ma), jnp.exp2(s[:, ATT_BLOCK:] - mb)], axis=1)
    vcat = jnp.concatenate([jnp.concatenate([sel_a(vb), sel_a(one)], axis=1),
                            jnp.concatenate([sel_b(vb), sel_b(one)], axis=1)], axis=0)
    pv = jnp.dot(p.astype(BF16), vcat, preferred_element_type=F32)
    return pv[:, :LANES], pv[:, LANES:], jnp.where(first_head, ma, mb)


def _dilated_kernel(q_ref, k_ref, v_ref, wg_ref, wu_ref, wd_ref, o_ref, wg_bf_ref, wu_bf_ref, wd_bf_ref,
                    bias_ref, bias2_ref, q4_ref, k4_ref, v4_ref, *scratch):
    @pl.when(pl.program_id(1) % 2 == 0)
    def _():
        for src, dst in ((wg_ref, wg_bf_ref), (wu_ref, wu_bf_ref), (wd_ref, wd_bf_ref)):
            dst[...] = src[...].astype(BF16)

    seq = q_ref.shape[0]
    blk = ATT_BLOCK
    sd = STAGE_DIL
    lc = seq // sd
    a = lax.broadcasted_iota(jnp.int32, (blk, 2 * blk), 0)
    kk = lax.broadcasted_iota(jnp.int32, (blk, 2 * blk), 1)
    dist = blk + a - kk
    bias_ref[...] = jnp.where((dist >= 0) & (dist <= blk), 0.0, -jnp.inf)
    causal = jnp.where(kk % blk <= a, 0.0, -jnp.inf)
    bias2_ref[...] = causal
    for r in range(sd):
        for src, dst in ((q_ref, q4_ref), (k_ref, k4_ref), (v_ref, v4_ref)):
            dst[r * lc:(r + 1) * lc, :] = src[pl.ds(r, lc, stride=sd), :]

    def view(p_idx, cls, b, n):
        dil = DILATED_PATTERNS[p_idx][1]
        if dil == 1:
            return (q_ref, k_ref, v_ref), pl.ds(b * blk, n)
        if dil == sd:
            return (q4_ref, k4_ref, v4_ref), pl.ds(cls * lc + b * blk, n)
        return (q4_ref, k4_ref, v4_ref), pl.ds((cls % sd) * lc + cls // sd + b * blk * sd, n, stride=sd)

    def run(tasks, first):
        loaded = []
        for p_idx, cls, b in tasks:
            (qr, kr, vr), qs = view(p_idx, cls, b, blk)
            _, ks = (None, qs) if first else view(p_idx, cls, b - 1, 2 * blk)
            loaded.append((qr[qs, :], kr[ks, :], vr[ks, :], qs))
        if first:
            bias2 = bias2_ref[...]
            results = [_attend_first(qb, kb, vb, bias2) for qb, kb, vb, _ in loaded]
        else:
            bias = bias_ref[...]
            results = [_attend(qb, kb, vb, bias) for qb, kb, vb, _ in loaded]
        for (p_idx, _, _), (_, _, _, dst), res in zip(tasks, loaded, results):
            for ref, val in zip(scratch[3 * p_idx:3 * p_idx + 3], res):
                ref[dst, :] = val

    n_blocks = [seq // dil // blk for _, dil in DILATED_PATTERNS]
    run([(p, cls, 0) for p, (_, dil) in enumerate(DILATED_PATTERNS) for cls in range(dil)], True)
    for p, (_, dil) in enumerate(DILATED_PATTERNS):
        if n_blocks[p] > 1:
            run([(p, cls, b) for b in range(1, n_blocks[p]) for cls in range(dil)], False)

    def combine(c, carry):
        for r in range(sd):
            nat = pl.ds(r + c * blk * sd, blk, stride=sd)
            cls = pl.ds(pl.multiple_of(r * lc + c * blk, blk), blk)
            sls = [nat if dil == 1 else cls for _, dil in DILATED_PATTERNS]
            ms = [scratch[3 * p + 2][sl, :] for p, sl in enumerate(sls)]
            m = functools.reduce(jnp.maximum, ms)
            ws = [jnp.exp2(mp - m) for mp in ms]
            num = functools.reduce(lambda x, y: x + y,
                                   [w * scratch[3 * p][sl, :] for p, (w, sl) in enumerate(zip(ws, sls))])
            den = functools.reduce(lambda x, y: x + y,
                                   [w * scratch[3 * p + 1][sl, :] for p, (w, sl) in enumerate(zip(ws, sls))])
            o_ref[nat, :] = num / den
        return carry

    lax.fori_loop(0, lc // blk, combine, 0)


def _dilated(aq, ak, av, weights, batch, seq):
    dils = [dil for _, dil in DILATED_PATTERNS]
    assert dils == [1, STAGE_DIL, STAGE_DIL * STAGE_DIL]
    for window, dil in DILATED_PATTERNS:
        assert window // dil == ATT_BLOCK and (seq // dil) % ATT_BLOCK == 0
    spec = pl.BlockSpec((seq, LANES), lambda b, hp: (b, hp))
    n_pairs = ATT_W // LANES
    n_slices = batch * n_pairs // 2
    w_specs = []
    for w in weights:
        rows = w.shape[0] // n_slices
        assert rows * n_slices == w.shape[0] and rows % BF16_SUBLANES == 0 and n_pairs % 2 == 0
        w_specs.append(pl.BlockSpec((rows, w.shape[1]), lambda b, hp: (b * (n_pairs // 2) + hp // 2, 0)))
    return pl.pallas_call(
        _dilated_kernel,
        out_shape=(jax.ShapeDtypeStruct((batch * seq, ATT_W), F32),)
                  + tuple(jax.ShapeDtypeStruct(w.shape, BF16) for w in weights),
        grid=(batch, n_pairs),
        in_specs=[spec, spec, spec] + w_specs,
        out_specs=(spec,) + tuple(w_specs),
        scratch_shapes=[pltpu.VMEM((ATT_BLOCK, 2 * ATT_BLOCK), F32) for _ in range(2)]
                       + [pltpu.VMEM((seq, LANES), F32) for _ in range(3 + 3 * len(DILATED_PATTERNS))],
        compiler_params=pltpu.CompilerParams(dimension_semantics=("arbitrary", "arbitrary"),
                                             vmem_limit_bytes=VMEM_LIMIT),
        name="dilated",
    )(aq, ak, av, *weights)


def _out_ffn_kernel(ret_ref, att_ref, x_ref, wo_ref, g1_ref, g2_ref, g3_ref, wg_ref, wu_ref, wd_ref,
                    o_ref):
    n_sub = x_ref.shape[0] // FFN_SUB_ROWS

    def head(s):
        rows = pl.ds(s * FFN_SUB_ROWS, FFN_SUB_ROWS)
        mix = (jnp.dot(ret_ref[rows, :], wo_ref[:RET_V_W, :], preferred_element_type=F32)
               + jnp.dot(att_ref[rows, :].astype(BF16), wo_ref[RET_V_W:, :], preferred_element_type=F32))
        x1 = x_ref[rows, :] + _rms(mix, g1_ref[...])
        return x1, _rms(x1, g2_ref[...]).astype(BF16)

    def swiglu(h):
        acc = None
        for c in range(N_FF_CHUNKS):
            cols = slice(c * FF_CHUNK, (c + 1) * FF_CHUNK)
            a = jnp.dot(h, wg_ref[:, cols], preferred_element_type=F32)
            u = jnp.dot(h, wu_ref[:, cols], preferred_element_type=F32)
            act = (a * jax.nn.sigmoid(a) * u).astype(BF16)
            d = jnp.dot(act, wd_ref[cols, :], preferred_element_type=F32)
            acc = d if acc is None else acc + d
        return acc

    cur = head(0)
    for s in range(n_sub):
        nxt = head(s + 1) if s + 1 < n_sub else None
        x1, h = cur
        o_ref[pl.ds(s * FFN_SUB_ROWS, FFN_SUB_ROWS), :] = x1 + _rms(swiglu(h), g3_ref[...])
        cur = nxt


def _out_ffn(ret, att, x2, wo, g1, g2, g3, wg, wu, wd, tm):
    t = x2.shape[0]
    row = lambda w: pl.BlockSpec((tm, w), lambda i: (i, 0))
    consts = (wo, g1, g2, g3, wg, wu, wd)
    return pl.pallas_call(
        _out_ffn_kernel,
        out_shape=jax.ShapeDtypeStruct((t, D_MODEL), F32),
        grid=(t // tm,),
        in_specs=[row(RET_V_W), row(ATT_W), row(D_MODEL)] + [_const_spec(c.shape) for c in consts],
        out_specs=row(D_MODEL),
        compiler_params=pltpu.CompilerParams(dimension_semantics=("arbitrary",),
                                             vmem_limit_bytes=VMEM_LIMIT),
        name="out_ffn",
    )(ret, att, x2, *consts)


def _layer(x2, pos2, batch, seq, w_in, w_out, g_pre_mix, g_post_mix, g_pre_ffn, g_post_ffn,
           w_gate, w_up, w_down):
    row = lambda g: g.reshape(1, D_MODEL).astype(F32)
    rq, rqd, rkt, rkdt, rv, rg, aq, ak, av = _in_proj(x2, pos2, row(g_pre_mix), w_in, IN_PROJ_ROWS)
    ret, wo_bf = _retention(rq, rqd, rkt, rkdt, rv, rg, w_out, batch, seq)
    att, wg_bf, wu_bf, wd_bf = _dilated(aq, ak, av, (w_gate, w_up, w_down), batch, seq)
    return _out_ffn(ret, att, x2, wo_bf, row(g_post_mix), row(g_pre_ffn), row(g_post_ffn),
                    wg_bf, wu_bf, wd_bf, FFN_ROWS)


def kernel(x, positions, w_in, w_out, g_pre_mix, g_post_mix, g_pre_ffn, g_post_ffn, w_gate, w_up, w_down):
    batch, seq, d = x.shape
    assert d == D_MODEL and seq % (ATT_BLOCK * DILATED_PATTERNS[-1][1]) == 0
    assert (batch * seq) % FFN_ROWS == 0 and (batch * seq) % IN_PROJ_ROWS == 0
    x2 = x.reshape(batch * seq, d)
    pos2 = positions.reshape(batch * seq // LANES, LANES)
    for l in range(w_in.shape[0]):
        x2 = _layer(x2, pos2, batch, seq, w_in[l], w_out[l], g_pre_mix[l], g_post_mix[l], g_pre_ffn[l],
                    g_post_ffn[l], w_gate[l], w_up[l], w_down[l])
    return x2.reshape(batch, seq, d)
```

```python
import functools

import numpy as np
import jax
import jax.numpy as jnp
from jax import lax
from jax.experimental import pallas as pl
from jax.experimental.pallas import tpu as pltpu

F32 = jnp.float32
BF16 = jnp.bfloat16

D_MODEL = 1024
RET_HEADS = 8
RET_QK = 32
RET_V = 64
RET_QK_W = RET_HEADS * RET_QK
RET_V_W = RET_HEADS * RET_V
RET_CHUNK = 128
RET_ROT_BASE = 10000.0
ATT_HEADS = 8
ATT_DIM = 64
ATT_W = ATT_HEADS * ATT_DIM
ATT_BLOCK = 128
DILATED_PATTERNS = ((128, 1), (512, 4), (2048, 16))
ROPE_THETA = 500000.0
ROPE_DIM = ATT_DIM // 4
D_FF = 2816
RMS_EPS = 1e-6
GN_EPS = 1e-5
PROJ_WIDTH = 2 * RET_QK_W + 2 * RET_V_W + 3 * ATT_W

LANES = 128
BF16_SUBLANES = 16
FF_CHUNK = 256
N_FF_CHUNKS = D_FF // FF_CHUNK
IN_PROJ_ROWS = 1024
IN_PROJ_SUB_ROWS = 512
FFN_ROWS = 1024
FFN_SUB_ROWS = 512
VMEM_LIMIT = 56 * 1024 * 1024
CAST_COLS = 256
LOG2E = 1.4426950408889634
RET_GROUP = 8
STAGE_DIL = 4

TRIG_LANES = 32
TRIG_PACK = LANES // TRIG_LANES
TRIG_ONE_LANE = 24


def _rms(x, g):
    return x * lax.rsqrt(jnp.mean(x * x, axis=-1, keepdims=True) + RMS_EPS) * g


def _rot_tables():
    ret_f = 1.0 / (RET_ROT_BASE ** jnp.linspace(0.0, 1.0, RET_QK // 2, dtype=F32))
    att_f = ROPE_THETA ** (-jnp.arange(0, ROPE_DIM, 2, dtype=F32) / ROPE_DIM)
    tok = jnp.concatenate([ret_f, att_f, jnp.zeros((TRIG_LANES - 24,), F32)])
    freq = jnp.stack([jnp.zeros((LANES,), F32).at[j * TRIG_LANES:(j + 1) * TRIG_LANES].set(tok)
                      for j in range(TRIG_PACK)])
    e = np.zeros((TRIG_PACK, 2 * LANES, 4 * LANES), np.float32)
    for j in range(TRIG_PACK):
        cos0, sin0 = j * TRIG_LANES, LANES + j * TRIG_LANES
        for c in range(LANES):
            d = c % RET_QK
            e[j, cos0 + d % 16, c] = 1.0
            e[j, sin0 + d % 16, LANES + c] = -1.0 if d < 16 else 1.0
            d = c % ATT_DIM
            if d < ROPE_DIM:
                e[j, cos0 + 16 + d % 8, 2 * LANES + c] = 1.0
                e[j, sin0 + 16 + d % 8, 3 * LANES + c] = -1.0 if d < 8 else 1.0
            else:
                e[j, cos0 + TRIG_ONE_LANE, 2 * LANES + c] = 1.0
    return freq, jnp.asarray(e, BF16)


def _rotate(x, cos, sin, half, group):
    lane = lax.broadcasted_iota(jnp.int32, x.shape, 1)
    up = pltpu.roll(x, LANES - half, 1)
    dn = pltpu.roll(x, half, 1)
    partner = jnp.where((lane % group) < half, up, dn)
    return x * cos + partner * sin


def _in_proj_kernel(x_ref, pos_ref, g_ref, w_ref, freq_ref, e_ref, qdec_ref, kdec_ref,
                    rq_ref, rqd_ref, rkt_ref, rkdt_ref, rv_ref, rg_ref, aq_ref, ak_ref, av_ref, wbf_ref):
    @pl.when(pl.program_id(0) == 0)
    def _():
        for c in range(0, PROJ_WIDTH, CAST_COLS):
            wbf_ref[:, c:c + CAST_COLS] = w_ref[:, c:c + CAST_COLS].astype(BF16)

    n_sub = x_ref.shape[0] // IN_PROJ_SUB_ROWS
    col = {}
    c0 = 0
    for name, width in (("rq", RET_QK_W), ("rk", RET_QK_W), ("rv", RET_V_W), ("rg", RET_V_W),
                        ("aq", ATT_W), ("ak", ATT_W), ("av", ATT_W)):
        col[name] = (c0, width)
        c0 += width

    def tables(s):
        assert IN_PROJ_SUB_ROWS == TRIG_PACK * LANES
        ang = None
        for j in range(TRIG_PACK):
            r = s * TRIG_PACK + j
            prow = pos_ref[r:r + 1, :].astype(F32)
            pj = jnp.broadcast_to(prow, (LANES, LANES)).T
            term = pj * freq_ref[j:j + 1, :]
            ang = term if ang is None else ang + term
        cs = jnp.concatenate([jnp.cos(ang), jnp.sin(ang)], axis=1)
        hi = cs.astype(BF16)
        lo = (cs - hi.astype(F32)).astype(BF16)
        return jnp.concatenate(
            [jnp.dot(hi, e_ref[j], preferred_element_type=F32) + jnp.dot(lo, e_ref[j], preferred_element_type=F32)
             for j in range(TRIG_PACK)], axis=0)

    def body(s):
        rows = pl.ds(s * IN_PROJ_SUB_ROWS, IN_PROJ_SUB_ROWS)
        h = _rms(x_ref[rows, :], g_ref[...]).astype(BF16)

        def proj(name):
            c, width = col[name]
            return jnp.dot(h, wbf_ref[:, c:c + width], preferred_element_type=F32)

        rv_ref[rows, :] = proj("rv").astype(BF16)
        tab = tables(s)
        cos_r, sin_r = tab[:, 0:LANES], tab[:, LANES:2 * LANES]
        cos_a, sin_a = tab[:, 2 * LANES:3 * LANES], tab[:, 3 * LANES:]
        for name, ref, cos, sin, half, group, scale, dec_ref, decayed_ref, transposed in (
                ("rq", rq_ref, cos_r, sin_r, RET_QK // 2, RET_QK, None, qdec_ref, rqd_ref, False),
                ("rk", rkt_ref, cos_r, sin_r, RET_QK // 2, RET_QK, RET_QK ** -0.5, kdec_ref, rkdt_ref, True),
                ("aq", aq_ref, cos_a, sin_a, ROPE_DIM // 2, ATT_DIM, ATT_DIM ** -0.5 * LOG2E, None, None, False),
                ("ak", ak_ref, cos_a, sin_a, ROPE_DIM // 2, ATT_DIM, None, None, None, False)):
            p = proj(name)
            for j in range(p.shape[1] // LANES):
                lanes = slice(j * LANES, (j + 1) * LANES)
                r = _rotate(p[:, lanes], cos, sin, half, group)
                if scale is not None:
                    r = r * scale
                outs = [(ref, r)] + ([(decayed_ref, r * dec_ref[:, lanes])] if dec_ref is not None else [])
                for dst, val in outs:
                    if transposed:
                        dst[lanes, rows] = val.T.astype(dst.dtype)
                    else:
                        dst[rows, lanes] = val.astype(dst.dtype)
        rg_ref[rows, :] = proj("rg").astype(BF16)
        av_ref[rows, :] = proj("av")

    for s in range(n_sub):
        body(s)


def _const_spec(shape):
    return pl.BlockSpec(shape, lambda *_: (0,) * len(shape), pipeline_mode=pl.Buffered(1))


def _in_proj(x2, pos2, g, w, tm):
    t = x2.shape[0]
    freq, e = _rot_tables()
    tables = _retention_tables()
    assert IN_PROJ_SUB_ROWS % RET_CHUNK == 0
    reps = IN_PROJ_SUB_ROWS // RET_CHUNK
    qdec, kdec = (jnp.asarray(np.tile(tables[n], (reps, 1))) for n in ("q_decay", "k_decay"))
    consts = (g, w, freq, e, qdec, kdec)
    row = lambda w: pl.BlockSpec((tm, w), lambda i: (i, 0))
    outs = [((t, RET_QK_W), BF16, row(RET_QK_W))] * 2
    outs += [((RET_QK_W, t), BF16, pl.BlockSpec((RET_QK_W, tm), lambda i: (0, i)))] * 2
    outs += [((t, RET_V_W), BF16, row(RET_V_W))] * 2 + [((t, ATT_W), F32, row(ATT_W))] * 3
    return pl.pallas_call(
        _in_proj_kernel,
        out_shape=tuple(jax.ShapeDtypeStruct(shape, dt) for shape, dt, _ in outs),
        grid=(t // tm,),
        in_specs=[row(D_MODEL), pl.BlockSpec((tm // LANES, LANES), lambda i: (i, 0))]
                 + [_const_spec(c.shape) for c in consts],
        out_specs=tuple(spec for _, _, spec in outs),
        scratch_shapes=[pltpu.VMEM(w.shape, BF16)],
        compiler_params=pltpu.CompilerParams(dimension_semantics=("arbitrary",),
                                             vmem_limit_bytes=VMEM_LIMIT),
        name="in_proj",
    )(x2, pos2, *consts)


def _retention_tables():
    h, c = RET_HEADS, RET_CHUNK
    f = np.float32
    log_g = np.log1p(-np.exp2(f(-5.0) - np.arange(h, dtype=f))).astype(f)
    idx = np.arange(c, dtype=f)
    diff = idx[:, None] - idx[None, :]
    inner = np.where(diff >= 0, np.exp(log_g[:, None, None] * np.maximum(diff, f(0.0))), f(0.0)).astype(f)
    q_decay = np.exp(log_g[:, None] * (idx + f(1.0))[None, :]).astype(f)
    k_decay = np.exp(log_g[:, None] * (f(c) - f(1.0) - idx)[None, :]).astype(f)
    chunk_decay = np.exp(log_g * f(c)).astype(f)
    rows = np.arange(2 * RET_QK)[:, None] // RET_QK
    cols = np.arange(2 * RET_V)[None, :] // RET_V
    gi = np.arange(2 * LANES) // RET_V
    return {
        "inner_decay": np.concatenate([inner[0::2], inner[1::2]], axis=-1),
        "q_decay": np.repeat(q_decay.T, RET_QK, axis=1),
        "k_decay": np.repeat(k_decay.T, RET_QK, axis=1),
        "chunk_decay": np.repeat(chunk_decay, RET_V)[None, :],
        "state_mask": (rows == cols).astype(f),
        "group_mean": (gi[:, None] == gi[None, :]).astype(f) / f(RET_V),
    }


def _retention_kernel(q_ref, qd_ref, kt_ref, kdt_ref, v_ref, g_ref, dec_ref, cd_ref, bd_ref, gn_ref, *rest):
    n_w = (len(rest) - 1) // 2
    o_ref = rest[n_w]
    for w_ref, w_bf_ref in zip(rest[:n_w], rest[n_w + 1:]):
        w_bf_ref[...] = w_ref[...].astype(BF16)
    n_chunks = q_ref.shape[0] // RET_CHUNK
    lane_v = lax.broadcasted_iota(jnp.int32, (RET_CHUNK, LANES), 1)
    pairs = range(RET_HEADS // 2)

    def rows(ref, c):
        return ref[c * RET_CHUNK:(c + 1) * RET_CHUNK, :]

    def two_heads_kt(kt, j):
        def only(lo):
            pieces = [(lo, None), (RET_QK, kt[lo:lo + RET_QK, :]), (RET_QK_W - lo - RET_QK, None)]
            return jnp.concatenate([jnp.zeros((n, RET_CHUNK), BF16) if p is None else p
                                    for n, p in pieces if n > 0], axis=0)
        return jnp.concatenate([only(2 * j * RET_QK), only((2 * j + 1) * RET_QK)], axis=1)

    def two_heads_v(v, j):
        vj = v[:, j * LANES:(j + 1) * LANES]
        zero = jnp.zeros_like(vj)
        return jnp.concatenate([jnp.where(lane_v < RET_V, vj, zero),
                                jnp.where(lane_v >= RET_V, vj, zero)], axis=0)

    def stage_scores(chunks):
        return [[jnp.dot(rows(q_ref, c), two_heads_kt(kt_ref[:, c * RET_CHUNK:(c + 1) * RET_CHUNK], j),
                         preferred_element_type=F32) for j in pairs] for c in chunks]

    pair_rows = 2 * RET_QK

    def stage_kv(chunks):
        return [[jnp.dot(kdt_ref[j * pair_rows:(j + 1) * pair_rows, c * RET_CHUNK:(c + 1) * RET_CHUNK],
                         rows(v_ref, c)[:, j * LANES:(j + 1) * LANES], preferred_element_type=F32)
                 for j in pairs] for c in chunks]

    def stage_pv(chunks, scores):
        return [jnp.concatenate(
            [jnp.dot((sc[j] * dec_ref[j]).astype(BF16), two_heads_v(rows(v_ref, c), j),
                     preferred_element_type=F32) for j in pairs], axis=1)
            for c, sc in zip(chunks, scores)]

    def block_diagonal(tiles):
        zero = jnp.zeros((pair_rows, LANES), BF16)
        return jnp.concatenate(
            [jnp.concatenate([t.astype(BF16) if i == j else zero for i in pairs], axis=1)
             for j, t in enumerate(tiles)], axis=0)

    def stage_recurrence(chunks, kvs, inners, state):
        outs = []
        for c, kv, inner in zip(chunks, kvs, inners):
            outs.append(inner + jnp.dot(rows(qd_ref, c), block_diagonal(state), preferred_element_type=F32))
            state = [s * cd_ref[:, j * LANES:(j + 1) * LANES] + kv[j] * bd_ref[...] for j, s in enumerate(state)]
        return outs, state

    def halves(x):
        return [x[:, t * 2 * LANES:(t + 1) * 2 * LANES] for t in range(2)]

    def stage_mean(outs):
        return [[o_t - jnp.dot(o_t.astype(BF16), gn_ref[...], preferred_element_type=F32) for o_t in halves(o)]
                for o in outs]

    def stage_var(ds):
        return [[jnp.dot((d * d).astype(BF16), gn_ref[...], preferred_element_type=F32) for d in dv] for dv in ds]

    def stage_store(chunks, ds, vars_):
        for c, dv, vv in zip(chunks, ds, vars_):
            y = jnp.concatenate([d * lax.rsqrt(var + GN_EPS) for d, var in zip(dv, vv)], axis=1)
            g = rows(g_ref, c).astype(F32)
            o_ref[c * RET_CHUNK:(c + 1) * RET_CHUNK, :] = (y * (g * jax.nn.sigmoid(g))).astype(BF16)

    assert n_chunks % RET_GROUP == 0
    groups = [list(range(g0, g0 + RET_GROUP)) for g0 in range(0, n_chunks, RET_GROUP)]
    state = [jnp.zeros((pair_rows, LANES), F32) for _ in pairs]
    pending = None
    sc, kv = stage_scores(groups[0]), stage_kv(groups[0])
    for gi, chunks in enumerate(groups):
        inner = stage_pv(chunks, sc)
        if gi + 1 < len(groups):
            nxt = (stage_scores(groups[gi + 1]), stage_kv(groups[gi + 1]))
        outs, state = stage_recurrence(chunks, kv, inner, state)
        if pending is not None:
            stage_store(pending[0], pending[1], stage_var(pending[1]))
        pending = (chunks, stage_mean(outs))
        if gi + 1 < len(groups):
            sc, kv = nxt
    stage_store(pending[0], pending[1], stage_var(pending[1]))


def _retention(rq, rqd, rkt, rkdt, rv, rg, weights, batch, seq):
    tables = _retention_tables()
    consts = tuple(jnp.asarray(tables[n]) for n in ("inner_decay", "chunk_decay", "state_mask"))
    consts += (jnp.asarray(tables["group_mean"], BF16),)
    seq_spec = lambda w: pl.BlockSpec((seq, w), lambda b: (b, 0))
    t_spec = pl.BlockSpec((RET_QK_W, seq), lambda b: (0, b))
    w_specs = []
    for w in weights:
        rows = w.shape[0] // batch
        assert rows * batch == w.shape[0] and rows % BF16_SUBLANES == 0
        w_specs.append(pl.BlockSpec((rows, w.shape[1]), lambda b: (b, 0)))
    return pl.pallas_call(
        _retention_kernel,
        out_shape=(jax.ShapeDtypeStruct((batch * seq, RET_V_W), BF16),)
                  + tuple(jax.ShapeDtypeStruct(w.shape, BF16) for w in weights),
        grid=(batch,),
        in_specs=[seq_spec(RET_QK_W)] * 2 + [t_spec] * 2 + [seq_spec(RET_V_W)] * 2
                 + [_const_spec(c.shape) for c in consts] + w_specs,
        out_specs=(seq_spec(RET_V_W),) + tuple(w_specs),
        compiler_params=pltpu.CompilerParams(dimension_semantics=("arbitrary",),
                                             vmem_limit_bytes=VMEM_LIMIT),
        name="retention",
    )(rq, rqd, rkt, rkdt, rv, rg, *consts, *weights)


def _attend(qb, kb, vb, bias):
    nk = kb.shape[0]
    lane = lax.broadcasted_iota(jnp.int32, (ATT_BLOCK, LANES), 1)
    first_head = lane < ATT_DIM
    kbf = kb.astype(BF16)
    vext = jnp.concatenate([vb.astype(BF16), jnp.ones((nk, LANES), BF16)], axis=1)
    res = []
    for head_lanes in (first_head, jnp.logical_not(first_head)):
        qm = jnp.where(head_lanes, qb, 0.0).astype(BF16)
        s = lax.dot_general(qm, kbf, (((1,), (1,)), ((), ())), preferred_element_type=F32) + bias
        m = jnp.max(s, axis=-1, keepdims=True)
        p = jnp.exp2(s - m)
        pv = jnp.dot(p.astype(BF16), vext, preferred_element_type=F32)
        res.append((pv[:, :LANES], pv[:, LANES:], m))
    return tuple(jnp.where(first_head, a, b) for a, b in zip(*res))


def _attend_first(qb, kb, vb, bias2):
    lane = lax.broadcasted_iota(jnp.int32, (ATT_BLOCK, LANES), 1)
    first_head = lane < ATT_DIM
    sel_a = lambda x: jnp.where(first_head, x, 0.0).astype(BF16)
    sel_b = lambda x: jnp.where(first_head, 0.0, x).astype(BF16)
    one = jnp.ones((ATT_BLOCK, LANES), F32)
    kcat = jnp.concatenate([sel_a(kb), sel_b(kb)], axis=0)
    s = lax.dot_general(qb.astype(BF16), kcat, (((1,), (1,)), ((), ())), preferred_element_type=F32) + bias2
    ma = jnp.max(s[:, :ATT_BLOCK], axis=-1, keepdims=True)
    mb = jnp.max(s[:, ATT_BLOCK:], axis=-1, keepdims=True)
    p = jnp.concatenate([jnp.exp2(s[:, :ATT_BLOCK] - ma), jnp.exp2(s[:, ATT_BLOCK:] - mb)], axis=1)
    vcat = jnp.concatenate([jnp.concatenate([sel_a(vb), sel_a(one)], axis=1),
                            jnp.concatenate([sel_b(vb), sel_b(one)], axis=1)], axis=0)
    pv = jnp.dot(p.astype(BF16), vcat, preferred_element_type=F32)
    return pv[:, :LANES], pv[:, LANES:], jnp.where(first_head, ma, mb)


def _dilated_kernel(q_ref, k_ref, v_ref, o_ref, bias_ref, bias2_ref, q4_ref, k4_ref, v4_ref, *scratch):
    seq = q_ref.shape[0]
    blk = ATT_BLOCK
    sd = STAGE_DIL
    lc = seq // sd
    a = lax.broadcasted_iota(jnp.int32, (blk, 2 * blk), 0)
    kk = lax.broadcasted_iota(jnp.int32, (blk, 2 * blk), 1)
    dist = blk + a - kk
    bias_ref[...] = jnp.where((dist >= 0) & (dist <= blk), 0.0, -jnp.inf)
    causal = jnp.where(kk % blk <= a, 0.0, -jnp.inf)
    bias2_ref[...] = causal
    for r in range(sd):
        for src, dst in ((q_ref, q4_ref), (k_ref, k4_ref), (v_ref, v4_ref)):
            dst[r * lc:(r + 1) * lc, :] = src[pl.ds(r, lc, stride=sd), :]

    def view(p_idx, cls, b, n):
        dil = DILATED_PATTERNS[p_idx][1]
        if dil == 1:
            return (q_ref, k_ref, v_ref), pl.ds(b * blk, n)
        if dil == sd:
            return (q4_ref, k4_ref, v4_ref), pl.ds(cls * lc + b * blk, n)
        return (q4_ref, k4_ref, v4_ref), pl.ds((cls % sd) * lc + cls // sd + b * blk * sd, n, stride=sd)

    def run(tasks, first):
        loaded = []
        for p_idx, cls, b in tasks:
            (qr, kr, vr), qs = view(p_idx, cls, b, blk)
            _, ks = (None, qs) if first else view(p_idx, cls, b - 1, 2 * blk)
            loaded.append((qr[qs, :], kr[ks, :], vr[ks, :], qs))
        if first:
            bias2 = bias2_ref[...]
            results = [_attend_first(qb, kb, vb, bias2) for qb, kb, vb, _ in loaded]
        else:
            bias = bias_ref[...]
            results = [_attend(qb, kb, vb, bias) for qb, kb, vb, _ in loaded]
        for (p_idx, _, _), (_, _, _, dst), res in zip(tasks, loaded, results):
            for ref, val in zip(scratch[3 * p_idx:3 * p_idx + 3], res):
                ref[dst, :] = val

    n_blocks = [seq // dil // blk for _, dil in DILATED_PATTERNS]
    run([(p, cls, 0) for p, (_, dil) in enumerate(DILATED_PATTERNS) for cls in range(dil)], True)
    for p, (_, dil) in enumerate(DILATED_PATTERNS):
        if n_blocks[p] > 1:
            run([(p, cls, b) for b in range(1, n_blocks[p]) for cls in range(dil)], False)

    def combine(c, carry):
        for r in range(sd):
            nat = pl.ds(r + c * blk * sd, blk, stride=sd)
            cls = pl.ds(pl.multiple_of(r * lc + c * blk, blk), blk)
            sls = [nat if dil == 1 else cls for _, dil in DILATED_PATTERNS]
            ms = [scratch[3 * p + 2][sl, :] for p, sl in enumerate(sls)]
            m = functools.reduce(jnp.maximum, ms)
            ws = [jnp.exp2(mp - m) for mp in ms]
            num = functools.reduce(lambda x, y: x + y,
                                   [w * scratch[3 * p][sl, :] for p, (w, sl) in enumerate(zip(ws, sls))])
            den = functools.reduce(lambda x, y: x + y,
                                   [w * scratch[3 * p + 1][sl, :] for p, (w, sl) in enumerate(zip(ws, sls))])
            o_ref[nat, :] = num / den
        return carry

    lax.fori_loop(0, lc // blk, combine, 0)


def _dilated(aq, ak, av, batch, seq):
    dils = [dil for _, dil in DILATED_PATTERNS]
    assert dils == [1, STAGE_DIL, STAGE_DIL * STAGE_DIL]
    for window, dil in DILATED_PATTERNS:
        assert window // dil == ATT_BLOCK and (seq // dil) % ATT_BLOCK == 0
    spec = pl.BlockSpec((seq, LANES), lambda b, hp: (b, hp))
    return pl.pallas_call(
        _dilated_kernel,
        out_shape=jax.ShapeDtypeStruct((batch * seq, ATT_W), F32),
        grid=(batch, ATT_W // LANES),
        in_specs=[spec, spec, spec],
        out_specs=spec,
        scratch_shapes=[pltpu.VMEM((ATT_BLOCK, 2 * ATT_BLOCK), F32) for _ in range(2)]
                       + [pltpu.VMEM((seq, LANES), F32) for _ in range(3 + 3 * len(DILATED_PATTERNS))],
        compiler_params=pltpu.CompilerParams(dimension_semantics=("arbitrary", "arbitrary"),
                                             vmem_limit_bytes=VMEM_LIMIT),
        name="dilated",
    )(aq, ak, av)


def _out_ffn_kernel(ret_ref, att_ref, x_ref, wo_ref, g1_ref, g2_ref, g3_ref, wg_ref, wu_ref, wd_ref,
                    o_ref):
    n_sub = x_ref.shape[0] // FFN_SUB_ROWS

    def head(s):
        rows = pl.ds(s * FFN_SUB_ROWS, FFN_SUB_ROWS)
        mix = (jnp.dot(ret_ref[rows, :], wo_ref[:RET_V_W, :], preferred_element_type=F32)
               + jnp.dot(att_ref[rows, :].astype(BF16), wo_ref[RET_V_W:, :], preferred_element_type=F32))
        x1 = x_ref[rows, :] + _rms(mix, g1_ref[...])
        return x1, _rms(x1, g2_ref[...]).astype(BF16)

    def swiglu(h):
        acc = None
        for c in range(N_FF_CHUNKS):
            cols = slice(c * FF_CHUNK, (c + 1) * FF_CHUNK)
            a = jnp.dot(h, wg_ref[:, cols], preferred_element_type=F32)
            u = jnp.dot(h, wu_ref[:, cols], preferred_element_type=F32)
            act = (a * jax.nn.sigmoid(a) * u).astype(BF16)
            d = jnp.dot(act, wd_ref[cols, :], preferred_element_type=F32)
            acc = d if acc is None else acc + d
        return acc

    cur = head(0)
    for s in range(n_sub):
        nxt = head(s + 1) if s + 1 < n_sub else None
        x1, h = cur
        o_ref[pl.ds(s * FFN_SUB_ROWS, FFN_SUB_ROWS), :] = x1 + _rms(swiglu(h), g3_ref[...])
        cur = nxt


def _out_ffn(ret, att, x2, wo, g1, g2, g3, wg, wu, wd, tm):
    t = x2.shape[0]
    row = lambda w: pl.BlockSpec((tm, w), lambda i: (i, 0))
    consts = (wo, g1, g2, g3, wg, wu, wd)
    return pl.pallas_call(
        _out_ffn_kernel,
        out_shape=jax.ShapeDtypeStruct((t, D_MODEL), F32),
        grid=(t // tm,),
        in_specs=[row(RET_V_W), row(ATT_W), row(D_MODEL)] + [_const_spec(c.shape) for c in consts],
        out_specs=row(D_MODEL),
        compiler_params=pltpu.CompilerParams(dimension_semantics=("arbitrary",),
                                             vmem_limit_bytes=VMEM_LIMIT),
        name="out_ffn",
    )(ret, att, x2, *consts)


def _layer(x2, pos2, batch, seq, w_in, w_out, g_pre_mix, g_post_mix, g_pre_ffn, g_post_ffn,
           w_gate, w_up, w_down):
    row = lambda g: g.reshape(1, D_MODEL).astype(F32)
    rq, rqd, rkt, rkdt, rv, rg, aq, ak, av = _in_proj(x2, pos2, row(g_pre_mix), w_in, IN_PROJ_ROWS)
    ret, wo_bf, wg_bf, wu_bf, wd_bf = _retention(rq, rqd, rkt, rkdt, rv, rg, (w_out, w_gate, w_up, w_down),
                                                 batch, seq)
    att = _dilated(aq, ak, av, batch, seq)
    return _out_ffn(ret, att, x2, wo_bf, row(g_post_mix), row(g_pre_ffn), row(g_post_ffn),
                    wg_bf, wu_bf, wd_bf, FFN_ROWS)


def kernel(x, positions, w_in, w_out, g_pre_mix, g_post_mix, g_pre_ffn, g_post_ffn, w_gate, w_up, w_down):
    batch, seq, d = x.shape
    assert d == D_MODEL and seq % (ATT_BLOCK * DILATED_PATTERNS[-1][1]) == 0
    assert (batch * seq) % FFN_ROWS == 0 and (batch * seq) % IN_PROJ_ROWS == 0
    x2 = x.reshape(batch * seq, d)
    pos2 = positions.reshape(batch * seq // LANES, LANES)
    for l in range(w_in.shape[0]):
        x2 = _layer(x2, pos2, batch, seq, w_in[l], w_out[l], g_pre_mix[l], g_post_mix[l], g_pre_ffn[l],
                    g_post_ffn[l], w_gate[l], w_up[l], w_down[l])
    return x2.reshape(batch, seq, d)
```

```python
import functools

import numpy as np
import jax
import jax.numpy as jnp
from jax import lax
from jax.experimental import pallas as pl
from jax.experimental.pallas import tpu as pltpu

F32 = jnp.float32
BF16 = jnp.bfloat16

D_MODEL = 1024
RET_HEADS = 8
RET_QK = 32
RET_V = 64
RET_QK_W = RET_HEADS * RET_QK
RET_V_W = RET_HEADS * RET_V
RET_CHUNK = 128
RET_ROT_BASE = 10000.0
ATT_HEADS = 8
ATT_DIM = 64
ATT_W = ATT_HEADS * ATT_DIM
ATT_BLOCK = 128
DILATED_PATTERNS = ((128, 1), (512, 4), (2048, 16))
ROPE_THETA = 500000.0
ROPE_DIM = ATT_DIM // 4
D_FF = 2816
RMS_EPS = 1e-6
GN_EPS = 1e-5
PROJ_WIDTH = 2 * RET_QK_W + 2 * RET_V_W + 3 * ATT_W

LANES = 128
BF16_SUBLANES = 16
FF_CHUNK = 256
N_FF_CHUNKS = D_FF // FF_CHUNK
IN_PROJ_ROWS = 1024
IN_PROJ_SUB_ROWS = 512
FFN_ROWS = 1024
FFN_SUB_ROWS = 512
VMEM_LIMIT = 56 * 1024 * 1024
CAST_COLS = 256
LOG2E = 1.4426950408889634
RET_GROUP = 8
STAGE_DIL = 4

TRIG_LANES = 32
TRIG_PACK = LANES // TRIG_LANES
TRIG_ONE_LANE = 24


def _rms(x, g):
    return x * lax.rsqrt(jnp.mean(x * x, axis=-1, keepdims=True) + RMS_EPS) * g


def _rot_freqs():
    ret_f = 1.0 / (RET_ROT_BASE ** jnp.linspace(0.0, 1.0, RET_QK // 2, dtype=F32))
    att_f = ROPE_THETA ** (-jnp.arange(0, ROPE_DIM, 2, dtype=F32) / ROPE_DIM)
    tok = jnp.concatenate([ret_f, att_f, jnp.zeros((TRIG_LANES - 24,), F32)])
    return jnp.stack([jnp.zeros((LANES,), F32).at[j * TRIG_LANES:(j + 1) * TRIG_LANES].set(tok)
                      for j in range(TRIG_PACK)])


def _rotate(x, cos, sin, half, group):
    lane = lax.broadcasted_iota(jnp.int32, x.shape, 1)
    up = pltpu.roll(x, LANES - half, 1)
    dn = pltpu.roll(x, half, 1)
    partner = jnp.where((lane % group) < half, up, dn)
    return x * cos + partner * sin


def _in_proj_kernel(x_ref, pos_ref, g_ref, w_ref, freq_ref, qdec_ref, kdec_ref,
                    rq_ref, rqd_ref, rkt_ref, rkdt_ref, rv_ref, rg_ref, aq_ref, ak_ref, av_ref, wbf_ref):
    @pl.when(pl.program_id(0) == 0)
    def _():
        for c in range(0, PROJ_WIDTH, CAST_COLS):
            wbf_ref[:, c:c + CAST_COLS] = w_ref[:, c:c + CAST_COLS].astype(BF16)

    n_sub = x_ref.shape[0] // IN_PROJ_SUB_ROWS
    col = {}
    c0 = 0
    for name, width in (("rq", RET_QK_W), ("rk", RET_QK_W), ("rv", RET_V_W), ("rg", RET_V_W),
                        ("aq", ATT_W), ("ak", ATT_W), ("av", ATT_W)):
        col[name] = (c0, width)
        c0 += width

    def tables(s):
        assert IN_PROJ_SUB_ROWS == TRIG_PACK * LANES
        ang = None
        for j in range(TRIG_PACK):
            r = s * TRIG_PACK + j
            prow = pos_ref[r:r + 1, :].astype(F32)
            pj = jnp.broadcast_to(prow, (LANES, LANES)).T
            term = pj * freq_ref[j:j + 1, :]
            ang = term if ang is None else ang + term
        cosv, sinv = jnp.cos(ang), jnp.sin(ang)
        lane = lax.broadcasted_iota(jnp.int32, (LANES, LANES), 1)
        src_r = (lane % RET_QK) % (RET_QK // 2)
        d_a = lane % ATT_DIM
        src_a = jnp.where(d_a < ROPE_DIM, RET_QK // 2 + d_a % (ROPE_DIM // 2), TRIG_ONE_LANE)
        sign_r = jnp.where(lane % RET_QK < RET_QK // 2, -1.0, 1.0)
        sign_a = jnp.where(d_a < ROPE_DIM // 2, -1.0, 1.0)
        blocks = []
        for j in range(TRIG_PACK):
            pick = lambda v, src: jnp.take_along_axis(v, src + j * TRIG_LANES, axis=1)
            blocks.append(jnp.concatenate([pick(cosv, src_r), pick(sinv, src_r) * sign_r,
                                           pick(cosv, src_a), pick(sinv, src_a) * sign_a], axis=1))
        return jnp.concatenate(blocks, axis=0)

    def body(s):
        rows = pl.ds(s * IN_PROJ_SUB_ROWS, IN_PROJ_SUB_ROWS)
        h = _rms(x_ref[rows, :], g_ref[...]).astype(BF16)

        def proj(name):
            c, width = col[name]
            return jnp.dot(h, wbf_ref[:, c:c + width], preferred_element_type=F32)

        rv_ref[rows, :] = proj("rv").astype(BF16)
        tab = tables(s)
        cos_r, sin_r = tab[:, 0:LANES], tab[:, LANES:2 * LANES]
        cos_a, sin_a = tab[:, 2 * LANES:3 * LANES], tab[:, 3 * LANES:]
        for name, ref, cos, sin, half, group, scale, dec_ref, decayed_ref, transposed in (
                ("rq", rq_ref, cos_r, sin_r, RET_QK // 2, RET_QK, None, qdec_ref, rqd_ref, False),
                ("rk", rkt_ref, cos_r, sin_r, RET_QK // 2, RET_QK, RET_QK ** -0.5, kdec_ref, rkdt_ref, True),
                ("aq", aq_ref, cos_a, sin_a, ROPE_DIM // 2, ATT_DIM, ATT_DIM ** -0.5 * LOG2E, None, None, False),
                ("ak", ak_ref, cos_a, sin_a, ROPE_DIM // 2, ATT_DIM, None, None, None, False)):
            p = proj(name)
            for j in range(p.shape[1] // LANES):
                lanes = slice(j * LANES, (j + 1) * LANES)
                r = _rotate(p[:, lanes], cos, sin, half, group)
                if scale is not None:
                    r = r * scale
                outs = [(ref, r)] + ([(decayed_ref, r * dec_ref[:, lanes])] if dec_ref is not None else [])
                for dst, val in outs:
                    if transposed:
                        dst[lanes, rows] = val.T.astype(dst.dtype)
                    else:
                        dst[rows, lanes] = val.astype(dst.dtype)
        rg_ref[rows, :] = proj("rg").astype(BF16)
        av_ref[rows, :] = proj("av")

    for s in range(n_sub):
        body(s)


def _const_spec(shape):
    return pl.BlockSpec(shape, lambda *_: (0,) * len(shape), pipeline_mode=pl.Buffered(1))


def _in_proj(x2, pos2, g, w, tm):
    t = x2.shape[0]
    freq = _rot_freqs()
    tables = _retention_tables()
    assert IN_PROJ_SUB_ROWS % RET_CHUNK == 0
    reps = IN_PROJ_SUB_ROWS // RET_CHUNK
    qdec, kdec = (jnp.asarray(np.tile(tables[n], (reps, 1))) for n in ("q_decay", "k_decay"))
    consts = (g, w, freq, qdec, kdec)
    row = lambda w: pl.BlockSpec((tm, w), lambda i: (i, 0))
    outs = [((t, RET_QK_W), BF16, row(RET_QK_W))] * 2
    outs += [((RET_QK_W, t), BF16, pl.BlockSpec((RET_QK_W, tm), lambda i: (0, i)))] * 2
    outs += [((t, RET_V_W), BF16, row(RET_V_W))] * 2 + [((t, ATT_W), F32, row(ATT_W))] * 3
    return pl.pallas_call(
        _in_proj_kernel,
        out_shape=tuple(jax.ShapeDtypeStruct(shape, dt) for shape, dt, _ in outs),
        grid=(t // tm,),
        in_specs=[row(D_MODEL), pl.BlockSpec((tm // LANES, LANES), lambda i: (i, 0))]
                 + [_const_spec(c.shape) for c in consts],
        out_specs=tuple(spec for _, _, spec in outs),
        scratch_shapes=[pltpu.VMEM(w.shape, BF16)],
        compiler_params=pltpu.CompilerParams(dimension_semantics=("arbitrary",),
                                             vmem_limit_bytes=VMEM_LIMIT),
        name="in_proj",
    )(x2, pos2, *consts)


def _retention_tables():
    h, c = RET_HEADS, RET_CHUNK
    f = np.float32
    log_g = np.log1p(-np.exp2(f(-5.0) - np.arange(h, dtype=f))).astype(f)
    idx = np.arange(c, dtype=f)
    diff = idx[:, None] - idx[None, :]
    inner = np.where(diff >= 0, np.exp(log_g[:, None, None] * np.maximum(diff, f(0.0))), f(0.0)).astype(f)
    q_decay = np.exp(log_g[:, None] * (idx + f(1.0))[None, :]).astype(f)
    k_decay = np.exp(log_g[:, None] * (f(c) - f(1.0) - idx)[None, :]).astype(f)
    chunk_decay = np.exp(log_g * f(c)).astype(f)
    rows = np.arange(2 * RET_QK)[:, None] // RET_QK
    cols = np.arange(2 * RET_V)[None, :] // RET_V
    gi = np.arange(2 * LANES) // RET_V
    return {
        "inner_decay": np.concatenate([inner[0::2], inner[1::2]], axis=-1),
        "q_decay": np.repeat(q_decay.T, RET_QK, axis=1),
        "k_decay": np.repeat(k_decay.T, RET_QK, axis=1),
        "chunk_decay": np.repeat(chunk_decay, RET_V)[None, :],
        "state_mask": (rows == cols).astype(f),
        "group_mean": (gi[:, None] == gi[None, :]).astype(f) / f(RET_V),
    }


def _retention_kernel(q_ref, qd_ref, kt_ref, kdt_ref, v_ref, g_ref, dec_ref, cd_ref, bd_ref, gn_ref, *rest):
    n_w = (len(rest) - 1) // 2
    o_ref = rest[n_w]
    for w_ref, w_bf_ref in zip(rest[:n_w], rest[n_w + 1:]):
        w_bf_ref[...] = w_ref[...].astype(BF16)
    n_chunks = q_ref.shape[0] // RET_CHUNK
    lane_v = lax.broadcasted_iota(jnp.int32, (RET_CHUNK, LANES), 1)
    pairs = range(RET_HEADS // 2)

    def rows(ref, c):
        return ref[c * RET_CHUNK:(c + 1) * RET_CHUNK, :]

    def two_heads_kt(kt, j):
        def only(lo):
            pieces = [(lo, None), (RET_QK, kt[lo:lo + RET_QK, :]), (RET_QK_W - lo - RET_QK, None)]
            return jnp.concatenate([jnp.zeros((n, RET_CHUNK), BF16) if p is None else p
                                    for n, p in pieces if n > 0], axis=0)
        return jnp.concatenate([only(2 * j * RET_QK), only((2 * j + 1) * RET_QK)], axis=1)

    def two_heads_v(v, j):
        vj = v[:, j * LANES:(j + 1) * LANES]
        zero = jnp.zeros_like(vj)
        return jnp.concatenate([jnp.where(lane_v < RET_V, vj, zero),
                                jnp.where(lane_v >= RET_V, vj, zero)], axis=0)

    def stage_scores(chunks):
        return [[jnp.dot(rows(q_ref, c), two_heads_kt(kt_ref[:, c * RET_CHUNK:(c + 1) * RET_CHUNK], j),
                         preferred_element_type=F32) for j in pairs] for c in chunks]

    pair_rows = 2 * RET_QK

    def stage_kv(chunks):
        return [[jnp.dot(kdt_ref[j * pair_rows:(j + 1) * pair_rows, c * RET_CHUNK:(c + 1) * RET_CHUNK],
                         rows(v_ref, c)[:, j * LANES:(j + 1) * LANES], preferred_element_type=F32)
                 for j in pairs] for c in chunks]

    def stage_pv(chunks, scores):
        return [jnp.concatenate(
            [jnp.dot((sc[j] * dec_ref[j]).astype(BF16), two_heads_v(rows(v_ref, c), j),
                     preferred_element_type=F32) for j in pairs], axis=1)
            for c, sc in zip(chunks, scores)]

    def block_diagonal(tiles):
        zero = jnp.zeros((pair_rows, LANES), BF16)
        return jnp.concatenate(
            [jnp.concatenate([t.astype(BF16) if i == j else zero for i in pairs], axis=1)
             for j, t in enumerate(tiles)], axis=0)

    def stage_recurrence(chunks, kvs, inners, state):
        outs = []
        for c, kv, inner in zip(chunks, kvs, inners):
            outs.append(inner + jnp.dot(rows(qd_ref, c), block_diagonal(state), preferred_element_type=F32))
            state = [s * cd_ref[:, j * LANES:(j + 1) * LANES] + kv[j] * bd_ref[...] for j, s in enumerate(state)]
        return outs, state

    def halves(x):
        return [x[:, t * 2 * LANES:(t + 1) * 2 * LANES] for t in range(2)]

    def stage_mean(outs):
        return [[o_t - jnp.dot(o_t.astype(BF16), gn_ref[...], preferred_element_type=F32) for o_t in halves(o)]
                for o in outs]

    def stage_var(ds):
        return [[jnp.dot((d * d).astype(BF16), gn_ref[...], preferred_element_type=F32) for d in dv] for dv in ds]

    def stage_store(chunks, ds, vars_):
        for c, dv, vv in zip(chunks, ds, vars_):
            y = jnp.concatenate([d * lax.rsqrt(var + GN_EPS) for d, var in zip(dv, vv)], axis=1)
            g = rows(g_ref, c).astype(F32)
            o_ref[c * RET_CHUNK:(c + 1) * RET_CHUNK, :] = (y * (g * jax.nn.sigmoid(g))).astype(BF16)

    assert n_chunks % RET_GROUP == 0
    groups = [list(range(g0, g0 + RET_GROUP)) for g0 in range(0, n_chunks, RET_GROUP)]
    state = [jnp.zeros((pair_rows, LANES), F32) for _ in pairs]
    pending = None
    sc, kv = stage_scores(groups[0]), stage_kv(groups[0])
    for gi, chunks in enumerate(groups):
        inner = stage_pv(chunks, sc)
        if gi + 1 < len(groups):
            nxt = (stage_scores(groups[gi + 1]), stage_kv(groups[gi + 1]))
        outs, state = stage_recurrence(chunks, kv, inner, state)
        if pending is not None:
            stage_store(pending[0], pending[1], stage_var(pending[1]))
        pending = (chunks, stage_mean(outs))
        if gi + 1 < len(groups):
            sc, kv = nxt
    stage_store(pending[0], pending[1], stage_var(pending[1]))


def _retention(rq, rqd, rkt, rkdt, rv, rg, weights, batch, seq):
    tables = _retention_tables()
    consts = tuple(jnp.asarray(tables[n]) for n in ("inner_decay", "chunk_decay", "state_mask"))
    consts += (jnp.asarray(tables["group_mean"], BF16),)
    seq_spec = lambda w: pl.BlockSpec((seq, w), lambda b: (b, 0))
    t_spec = pl.BlockSpec((RET_QK_W, seq), lambda b: (0, b))
    w_specs = []
    for w in weights:
        rows = w.shape[0] // batch
        assert rows * batch == w.shape[0] and rows % BF16_SUBLANES == 0
        w_specs.append(pl.BlockSpec((rows, w.shape[1]), lambda b: (b, 0)))
    return pl.pallas_call(
        _retention_kernel,
        out_shape=(jax.ShapeDtypeStruct((batch * seq, RET_V_W), BF16),)
                  + tuple(jax.ShapeDtypeStruct(w.shape, BF16) for w in weights),
        grid=(batch,),
        in_specs=[seq_spec(RET_QK_W)] * 2 + [t_spec] * 2 + [seq_spec(RET_V_W)] * 2
                 + [_const_spec(c.shape) for c in consts] + w_specs,
        out_specs=(seq_spec(RET_V_W),) + tuple(w_specs),
        compiler_params=pltpu.CompilerParams(dimension_semantics=("arbitrary",),
                                             vmem_limit_bytes=VMEM_LIMIT),
        name="retention",
    )(rq, rqd, rkt, rkdt, rv, rg, *consts, *weights)


def _attend(qb, kb, vb, bias):
    nk = kb.shape[0]
    lane = lax.broadcasted_iota(jnp.int32, (ATT_BLOCK, LANES), 1)
    first_head = lane < ATT_DIM
    kbf = kb.astype(BF16)
    vext = jnp.concatenate([vb.astype(BF16), jnp.ones((nk, LANES), BF16)], axis=1)
    res = []
    for head_lanes in (first_head, jnp.logical_not(first_head)):
        qm = jnp.where(head_lanes, qb, 0.0).astype(BF16)
        s = lax.dot_general(qm, kbf, (((1,), (1,)), ((), ())), preferred_element_type=F32) + bias
        m = jnp.max(s, axis=-1, keepdims=True)
        p = jnp.exp2(s - m)
        pv = jnp.dot(p.astype(BF16), vext, preferred_element_type=F32)
        res.append((pv[:, :LANES], pv[:, LANES:], m))
    return tuple(jnp.where(first_head, a, b) for a, b in zip(*res))


def _attend_first(qb, kb, vb, bias2):
    lane = lax.broadcasted_iota(jnp.int32, (ATT_BLOCK, LANES), 1)
    first_head = lane < ATT_DIM
    sel_a = lambda x: jnp.where(first_head, x, 0.0).astype(BF16)
    sel_b = lambda x: jnp.where(first_head, 0.0, x).astype(BF16)
    one = jnp.ones((ATT_BLOCK, LANES), F32)
    kcat = jnp.concatenate([sel_a(kb), sel_b(kb)], axis=0)
    s = lax.dot_general(qb.astype(BF16), kcat, (((1,), (1,)), ((), ())), preferred_element_type=F32) + bias2
    ma = jnp.max(s[:, :ATT_BLOCK], axis=-1, keepdims=True)
    mb = jnp.max(s[:, ATT_BLOCK:], axis=-1, keepdims=True)
    p = jnp.concatenate([jnp.exp2(s[:, :ATT_BLOCK] - ma), jnp.exp2(s[:, ATT_BLOCK:] - mb)], axis=1)
    vcat = jnp.concatenate([jnp.concatenate([sel_a(vb), sel_a(one)], axis=1),
                            jnp.concatenate([sel_b(vb), sel_b(one)], axis=1)], axis=0)
    pv = jnp.dot(p.astype(BF16), vcat, preferred_element_type=F32)
    return pv[:, :LANES], pv[:, LANES:], jnp.where(first_head, ma, mb)


def _dilated_kernel(q_ref, k_ref, v_ref, o_ref, bias_ref, bias2_ref, q4_ref, k4_ref, v4_ref, *scratch):
    seq = q_ref.shape[0]
    blk = ATT_BLOCK
    sd = STAGE_DIL
    lc = seq // sd
    a = lax.broadcasted_iota(jnp.int32, (blk, 2 * blk), 0)
    kk = lax.broadcasted_iota(jnp.int32, (blk, 2 * blk), 1)
    dist = blk + a - kk
    bias_ref[...] = jnp.where((dist >= 0) & (dist <= blk), 0.0, -jnp.inf)
    causal = jnp.where(kk % blk <= a, 0.0, -jnp.inf)
    bias2_ref[...] = causal
    for r in range(sd):
        for src, dst in ((q_ref, q4_ref), (k_ref, k4_ref), (v_ref, v4_ref)):
            dst[r * lc:(r + 1) * lc, :] = src[pl.ds(r, lc, stride=sd), :]

    def view(p_idx, cls, b, n):
        dil = DILATED_PATTERNS[p_idx][1]
        if dil == 1:
            return (q_ref, k_ref, v_ref), pl.ds(b * blk, n)
        if dil == sd:
            return (q4_ref, k4_ref, v4_ref), pl.ds(cls * lc + b * blk, n)
        return (q4_ref, k4_ref, v4_ref), pl.ds((cls % sd) * lc + cls // sd + b * blk * sd, n, stride=sd)

    def run(tasks, first):
        loaded = []
        for p_idx, cls, b in tasks:
            (qr, kr, vr), qs = view(p_idx, cls, b, blk)
            _, ks = (None, qs) if first else view(p_idx, cls, b - 1, 2 * blk)
            loaded.append((qr[qs, :], kr[ks, :], vr[ks, :], qs))
        if first:
            bias2 = bias2_ref[...]
            results = [_attend_first(qb, kb, vb, bias2) for qb, kb, vb, _ in loaded]
        else:
            bias = bias_ref[...]
            results = [_attend(qb, kb, vb, bias) for qb, kb, vb, _ in loaded]
        for (p_idx, _, _), (_, _, _, dst), res in zip(tasks, loaded, results):
            for ref, val in zip(scratch[3 * p_idx:3 * p_idx + 3], res):
                ref[dst, :] = val

    n_blocks = [seq // dil // blk for _, dil in DILATED_PATTERNS]
    run([(p, cls, 0) for p, (_, dil) in enumerate(DILATED_PATTERNS) for cls in range(dil)], True)
    for p, (_, dil) in enumerate(DILATED_PATTERNS):
        if n_blocks[p] > 1:
            run([(p, cls, b) for b in range(1, n_blocks[p]) for cls in range(dil)], False)

    def combine(c, carry):
        for r in range(sd):
            nat = pl.ds(r + c * blk * sd, blk, stride=sd)
            cls = pl.ds(pl.multiple_of(r * lc + c * blk, blk), blk)
            sls = [nat if dil == 1 else cls for _, dil in DILATED_PATTERNS]
            ms = [scratch[3 * p + 2][sl, :] for p, sl in enumerate(sls)]
            m = functools.reduce(jnp.maximum, ms)
            ws = [jnp.exp2(mp - m) for mp in ms]
            num = functools.reduce(lambda x, y: x + y,
                                   [w * scratch[3 * p][sl, :] for p, (w, sl) in enumerate(zip(ws, sls))])
            den = functools.reduce(lambda x, y: x + y,
                                   [w * scratch[3 * p + 1][sl, :] for p, (w, sl) in enumerate(zip(ws, sls))])
            o_ref[nat, :] = num / den
        return carry

    lax.fori_loop(0, lc // blk, combine, 0)


def _dilated(aq, ak, av, batch, seq):
    dils = [dil for _, dil in DILATED_PATTERNS]
    assert dils == [1, STAGE_DIL, STAGE_DIL * STAGE_DIL]
    for window, dil in DILATED_PATTERNS:
        assert window // dil == ATT_BLOCK and (seq // dil) % ATT_BLOCK == 0
    spec = pl.BlockSpec((seq, LANES), lambda b, hp: (b, hp))
    return pl.pallas_call(
        _dilated_kernel,
        out_shape=jax.ShapeDtypeStruct((batch * seq, ATT_W), F32),
        grid=(batch, ATT_W // LANES),
        in_specs=[spec, spec, spec],
        out_specs=spec,
        scratch_shapes=[pltpu.VMEM((ATT_BLOCK, 2 * ATT_BLOCK), F32) for _ in range(2)]
                       + [pltpu.VMEM((seq, LANES), F32) for _ in range(3 + 3 * len(DILATED_PATTERNS))],
        compiler_params=pltpu.CompilerParams(dimension_semantics=("arbitrary", "arbitrary"),
                                             vmem_limit_bytes=VMEM_LIMIT),
        name="dilated",
    )(aq, ak, av)


def _out_ffn_kernel(ret_ref, att_ref, x_ref, wo_ref, g1_ref, g2_ref, g3_ref, wg_ref, wu_ref, wd_ref,
                    o_ref):
    n_sub = x_ref.shape[0] // FFN_SUB_ROWS

    def head(s):
        rows = pl.ds(s * FFN_SUB_ROWS, FFN_SUB_ROWS)
        mix = (jnp.dot(ret_ref[rows, :], wo_ref[:RET_V_W, :], preferred_element_type=F32)
               + jnp.dot(att_ref[rows, :].astype(BF16), wo_ref[RET_V_W:, :], preferred_element_type=F32))
        x1 = x_ref[rows, :] + _rms(mix, g1_ref[...])
        return x1, _rms(x1, g2_ref[...]).astype(BF16)

    def swiglu(h):
        acc = None
        for c in range(N_FF_CHUNKS):
            cols = slice(c * FF_CHUNK, (c + 1) * FF_CHUNK)
            a = jnp.dot(h, wg_ref[:, cols], preferred_element_type=F32)
            u = jnp.dot(h, wu_ref[:, cols], preferred_element_type=F32)
            act = (a * jax.nn.sigmoid(a) * u).astype(BF16)
            d = jnp.dot(act, wd_ref[cols, :], preferred_element_type=F32)
            acc = d if acc is None else acc + d
        return acc

    cur = head(0)
    for s in range(n_sub):
        nxt = head(s + 1) if s + 1 < n_sub else None
        x1, h = cur
        o_ref[pl.ds(s * FFN_SUB_ROWS, FFN_SUB_ROWS), :] = x1 + _rms(swiglu(h), g3_ref[...])
        cur = nxt


def _out_ffn(ret, att, x2, wo, g1, g2, g3, wg, wu, wd, tm):
    t = x2.shape[0]
    row = lambda w: pl.BlockSpec((tm, w), lambda i: (i, 0))
    consts = (wo, g1, g2, g3, wg, wu, wd)
    return pl.pallas_call(
        _out_ffn_kernel,
        out_shape=jax.ShapeDtypeStruct((t, D_MODEL), F32),
        grid=(t // tm,),
        in_specs=[row(RET_V_W), row(ATT_W), row(D_MODEL)] + [_const_spec(c.shape) for c in consts],
        out_specs=row(D_MODEL),
        compiler_params=pltpu.CompilerParams(dimension_semantics=("arbitrary",),
                                             vmem_limit_bytes=VMEM_LIMIT),
        name="out_ffn",
    )(ret, att, x2, *consts)


def _layer(x2, pos2, batch, seq, w_in, w_out, g_pre_mix, g_post_mix, g_pre_ffn, g_post_ffn,
           w_gate, w_up, w_down):
    row = lambda g: g.reshape(1, D_MODEL).astype(F32)
    rq, rqd, rkt, rkdt, rv, rg, aq, ak, av = _in_proj(x2, pos2, row(g_pre_mix), w_in, IN_PROJ_ROWS)
    ret, wo_bf, wg_bf, wu_bf, wd_bf = _retention(rq, rqd, rkt, rkdt, rv, rg, (w_out, w_gate, w_up, w_down),
                                                 batch, seq)
    att = _dilated(aq, ak, av, batch, seq)
    return _out_ffn(ret, att, x2, wo_bf, row(g_post_mix), row(g_pre_ffn), row(g_post_ffn),
                    wg_bf, wu_bf, wd_bf, FFN_ROWS)


def kernel(x, positions, w_in, w_out, g_pre_mix, g_post_mix, g_pre_ffn, g_post_ffn, w_gate, w_up, w_down):
    batch, seq, d = x.shape
    assert d == D_MODEL and seq % (ATT_BLOCK * DILATED_PATTERNS[-1][1]) == 0
    assert (batch * seq) % FFN_ROWS == 0 and (batch * seq) % IN_PROJ_ROWS == 0
    x2 = x.reshape(batch * seq, d)
    pos2 = positions.reshape(batch * seq // LANES, LANES)
    for l in range(w_in.shape[0]):
        x2 = _layer(x2, pos2, batch, seq, w_in[l], w_out[l], g_pre_mix[l], g_post_mix[l], g_pre_ffn[l],
                    g_post_ffn[l], w_gate[l], w_up[l], w_down[l])
    return x2.reshape(batch, seq, d)
```

```python
import functools

import numpy as np
import jax
import jax.numpy as jnp
from jax import lax
from jax.experimental import pallas as pl
from jax.experimental.pallas import tpu as pltpu

F32 = jnp.float32
BF16 = jnp.bfloat16

D_MODEL = 1024
RET_HEADS = 8
RET_QK = 32
RET_V = 64
RET_QK_W = RET_HEADS * RET_QK
RET_V_W = RET_HEADS * RET_V
RET_CHUNK = 128
RET_ROT_BASE = 10000.0
ATT_HEADS = 8
ATT_DIM = 64
ATT_W = ATT_HEADS * ATT_DIM
ATT_BLOCK = 128
DILATED_PATTERNS = ((128, 1), (512, 4), (2048, 16))
ROPE_THETA = 500000.0
ROPE_DIM = ATT_DIM // 4
D_FF = 2816
RMS_EPS = 1e-6
GN_EPS = 1e-5
PROJ_WIDTH = 2 * RET_QK_W + 2 * RET_V_W + 3 * ATT_W

LANES = 128
BF16_SUBLANES = 16
FF_CHUNK = 256
N_FF_CHUNKS = D_FF // FF_CHUNK
IN_PROJ_ROWS = 1024
IN_PROJ_SUB_ROWS = 512
FFN_ROWS = 1024
FFN_SUB_ROWS = 512
VMEM_LIMIT = 56 * 1024 * 1024
CAST_COLS = 256
LOG2E = 1.4426950408889634
RET_GROUP = 8
STAGE_DIL = 4

TRIG_LANES = 32
TRIG_PACK = LANES // TRIG_LANES
TRIG_ONE_LANE = RET_QK // 2 + ROPE_DIM // 2


def _rms(x, g):
    return x * lax.rsqrt(jnp.mean(x * x, axis=-1, keepdims=True) + RMS_EPS) * g


def _rot_freqs():
    ret_f = 1.0 / (RET_ROT_BASE ** jnp.linspace(0.0, 1.0, RET_QK // 2, dtype=F32))
    att_f = ROPE_THETA ** (-jnp.arange(0, ROPE_DIM, 2, dtype=F32) / ROPE_DIM)
    tok = jnp.concatenate([ret_f, att_f, jnp.zeros((TRIG_LANES - TRIG_ONE_LANE,), F32)])
    return jnp.stack([jnp.zeros((LANES,), F32).at[j * TRIG_LANES:(j + 1) * TRIG_LANES].set(tok)
                      for j in range(TRIG_PACK)])


def _rotate(x, cos, sin, half, group):
    lane = lax.broadcasted_iota(jnp.int32, x.shape, 1)
    up = pltpu.roll(x, LANES - half, 1)
    dn = pltpu.roll(x, half, 1)
    partner = jnp.where((lane % group) < half, up, dn)
    return x * cos + partner * sin


def _in_proj_kernel(x_ref, pos_ref, g_ref, w_ref, freq_ref, qdec_ref, kdec_ref,
                    rq_ref, rqd_ref, rkt_ref, rkdt_ref, rv_ref, rg_ref, aq_ref, ak_ref, av_ref, wbf_ref):
    @pl.when(pl.program_id(0) == 0)
    def _():
        for c in range(0, PROJ_WIDTH, CAST_COLS):
            wbf_ref[:, c:c + CAST_COLS] = w_ref[:, c:c + CAST_COLS].astype(BF16)

    n_sub = x_ref.shape[0] // IN_PROJ_SUB_ROWS
    col = {}
    c0 = 0
    for name, width in (("rq", RET_QK_W), ("rk", RET_QK_W), ("rv", RET_V_W), ("rg", RET_V_W),
                        ("aq", ATT_W), ("ak", ATT_W), ("av", ATT_W)):
        col[name] = (c0, width)
        c0 += width

    def tables(s):
        assert IN_PROJ_SUB_ROWS == TRIG_PACK * LANES
        ang = None
        for j in range(TRIG_PACK):
            r = s * TRIG_PACK + j
            prow = pos_ref[r:r + 1, :].astype(F32)
            pj = jnp.broadcast_to(prow, (LANES, LANES)).T
            term = pj * freq_ref[j:j + 1, :]
            ang = term if ang is None else ang + term
        cosv, sinv = jnp.cos(ang), jnp.sin(ang)
        lane = lax.broadcasted_iota(jnp.int32, (LANES, LANES), 1)
        src_r = (lane % RET_QK) % (RET_QK // 2)
        d_a = lane % ATT_DIM
        src_a = jnp.where(d_a < ROPE_DIM, RET_QK // 2 + d_a % (ROPE_DIM // 2), TRIG_ONE_LANE)
        sign_r = jnp.where(lane % RET_QK < RET_QK // 2, -1.0, 1.0)
        sign_a = jnp.where(d_a < ROPE_DIM // 2, -1.0, 1.0)
        blocks = []
        for j in range(TRIG_PACK):
            pick = lambda v, src: jnp.take_along_axis(v, src + j * TRIG_LANES, axis=1)
            blocks.append(jnp.concatenate([pick(cosv, src_r), pick(sinv, src_r) * sign_r,
                                           pick(cosv, src_a), pick(sinv, src_a) * sign_a], axis=1))
        return jnp.concatenate(blocks, axis=0)

    def body(s):
        rows = pl.ds(s * IN_PROJ_SUB_ROWS, IN_PROJ_SUB_ROWS)
        h = _rms(x_ref[rows, :], g_ref[...]).astype(BF16)

        def proj(name):
            c, width = col[name]
            return jnp.dot(h, wbf_ref[:, c:c + width], preferred_element_type=F32)

        rv_ref[rows, :] = proj("rv").astype(BF16)
        tab = tables(s)
        cos_r, sin_r = tab[:, 0:LANES], tab[:, LANES:2 * LANES]
        cos_a, sin_a = tab[:, 2 * LANES:3 * LANES], tab[:, 3 * LANES:]
        for name, ref, cos, sin, half, group, scale, dec_ref, decayed_ref, transposed in (
                ("rq", rq_ref, cos_r, sin_r, RET_QK // 2, RET_QK, None, qdec_ref, rqd_ref, False),
                ("rk", rkt_ref, cos_r, sin_r, RET_QK // 2, RET_QK, RET_QK ** -0.5, kdec_ref, rkdt_ref, True),
                ("aq", aq_ref, cos_a, sin_a, ROPE_DIM // 2, ATT_DIM, ATT_DIM ** -0.5 * LOG2E, None, None, False),
                ("ak", ak_ref, cos_a, sin_a, ROPE_DIM // 2, ATT_DIM, None, None, None, False)):
            p = proj(name)
            for j in range(p.shape[1] // LANES):
                lanes = slice(j * LANES, (j + 1) * LANES)
                r = _rotate(p[:, lanes], cos, sin, half, group)
                if scale is not None:
                    r = r * scale
                outs = [(ref, r)] + ([(decayed_ref, r * dec_ref[:, lanes])] if dec_ref is not None else [])
                for dst, val in outs:
                    if transposed:
                        dst[lanes, rows] = val.T.astype(dst.dtype)
                    else:
                        dst[rows, lanes] = val.astype(dst.dtype)
        rg_ref[rows, :] = proj("rg").astype(BF16)
        av_ref[rows, :] = proj("av")

    for s in range(n_sub):
        body(s)


def _const_spec(shape):
    return pl.BlockSpec(shape, lambda *_: (0,) * len(shape), pipeline_mode=pl.Buffered(1))


def _in_proj(x2, pos2, g, w, tm):
    t = x2.shape[0]
    freq = _rot_freqs()
    tables = _retention_tables()
    assert IN_PROJ_SUB_ROWS % RET_CHUNK == 0
    reps = IN_PROJ_SUB_ROWS // RET_CHUNK
    qdec, kdec = (jnp.asarray(np.tile(tables[n], (reps, 1))) for n in ("q_decay", "k_decay"))
    consts = (g, w, freq, qdec, kdec)
    row = lambda w: pl.BlockSpec((tm, w), lambda i: (i, 0))
    outs = [((t, RET_QK_W), BF16, row(RET_QK_W))] * 2
    outs += [((RET_QK_W, t), BF16, pl.BlockSpec((RET_QK_W, tm), lambda i: (0, i)))] * 2
    outs += [((t, RET_V_W), BF16, row(RET_V_W))] * 2 + [((t, ATT_W), F32, row(ATT_W))] * 3
    return pl.pallas_call(
        _in_proj_kernel,
        out_shape=tuple(jax.ShapeDtypeStruct(shape, dt) for shape, dt, _ in outs),
        grid=(t // tm,),
        in_specs=[row(D_MODEL), pl.BlockSpec((tm // LANES, LANES), lambda i: (i, 0))]
                 + [_const_spec(c.shape) for c in consts],
        out_specs=tuple(spec for _, _, spec in outs),
        scratch_shapes=[pltpu.VMEM(w.shape, BF16)],
        compiler_params=pltpu.CompilerParams(dimension_semantics=("arbitrary",),
                                             vmem_limit_bytes=VMEM_LIMIT),
        name="in_proj",
    )(x2, pos2, *consts)


def _retention_tables():
    h, c = RET_HEADS, RET_CHUNK
    f = np.float32
    log_g = np.log1p(-np.exp2(f(-5.0) - np.arange(h, dtype=f))).astype(f)
    idx = np.arange(c, dtype=f)
    diff = idx[:, None] - idx[None, :]
    inner = np.where(diff >= 0, np.exp(log_g[:, None, None] * np.maximum(diff, f(0.0))), f(0.0)).astype(f)
    q_decay = np.exp(log_g[:, None] * (idx + f(1.0))[None, :]).astype(f)
    k_decay = np.exp(log_g[:, None] * (f(c) - f(1.0) - idx)[None, :]).astype(f)
    chunk_decay = np.exp(log_g * f(c)).astype(f)
    rows = np.arange(2 * RET_QK)[:, None] // RET_QK
    cols = np.arange(2 * RET_V)[None, :] // RET_V
    gi = np.arange(2 * LANES) // RET_V
    return {
        "inner_decay": np.concatenate([inner[0::2], inner[1::2]], axis=-1),
        "q_decay": np.repeat(q_decay.T, RET_QK, axis=1),
        "k_decay": np.repeat(k_decay.T, RET_QK, axis=1),
        "chunk_decay": np.repeat(chunk_decay, RET_V)[None, :],
        "state_mask": (rows == cols).astype(f),
        "group_mean": (gi[:, None] == gi[None, :]).astype(f) / f(RET_V),
    }


def _retention_kernel(q_ref, qd_ref, kt_ref, kdt_ref, v_ref, g_ref, dec_ref, cd_ref, bd_ref, gn_ref, *rest):
    n_w = (len(rest) - 1) // 2
    o_ref = rest[n_w]
    for w_ref, w_bf_ref in zip(rest[:n_w], rest[n_w + 1:]):
        w_bf_ref[...] = w_ref[...].astype(BF16)
    n_chunks = q_ref.shape[0] // RET_CHUNK
    lane_v = lax.broadcasted_iota(jnp.int32, (RET_CHUNK, LANES), 1)
    pairs = range(RET_HEADS // 2)

    def rows(ref, c):
        return ref[c * RET_CHUNK:(c + 1) * RET_CHUNK, :]

    def two_heads_kt(kt, j):
        def only(lo):
            pieces = [(lo, None), (RET_QK, kt[lo:lo + RET_QK, :]), (RET_QK_W - lo - RET_QK, None)]
            return jnp.concatenate([jnp.zeros((n, RET_CHUNK), BF16) if p is None else p
                                    for n, p in pieces if n > 0], axis=0)
        return jnp.concatenate([only(2 * j * RET_QK), only((2 * j + 1) * RET_QK)], axis=1)

    def two_heads_v(v, j):
        vj = v[:, j * LANES:(j + 1) * LANES]
        zero = jnp.zeros_like(vj)
        return jnp.concatenate([jnp.where(lane_v < RET_V, vj, zero),
                                jnp.where(lane_v >= RET_V, vj, zero)], axis=0)

    def stage_scores(chunks):
        return [[jnp.dot(rows(q_ref, c), two_heads_kt(kt_ref[:, c * RET_CHUNK:(c + 1) * RET_CHUNK], j),
                         preferred_element_type=F32) for j in pairs] for c in chunks]

    pair_rows = 2 * RET_QK

    def stage_kv(chunks):
        return [[jnp.dot(kdt_ref[j * pair_rows:(j + 1) * pair_rows, c * RET_CHUNK:(c + 1) * RET_CHUNK],
                         rows(v_ref, c)[:, j * LANES:(j + 1) * LANES], preferred_element_type=F32)
                 for j in pairs] for c in chunks]

    def stage_pv(chunks, scores):
        return [jnp.concatenate(
            [jnp.dot((sc[j] * dec_ref[j]).astype(BF16), two_heads_v(rows(v_ref, c), j),
                     preferred_element_type=F32) for j in pairs], axis=1)
            for c, sc in zip(chunks, scores)]

    def block_diagonal(tiles):
        zero = jnp.zeros((pair_rows, LANES), BF16)
        return jnp.concatenate(
            [jnp.concatenate([t.astype(BF16) if i == j else zero for i in pairs], axis=1)
             for j, t in enumerate(tiles)], axis=0)

    def stage_recurrence(chunks, kvs, inners, state):
        outs = []
        for c, kv, inner in zip(chunks, kvs, inners):
            outs.append(inner + jnp.dot(rows(qd_ref, c), block_diagonal(state), preferred_element_type=F32))
            state = [s * cd_ref[:, j * LANES:(j + 1) * LANES] + kv[j] * bd_ref[...] for j, s in enumerate(state)]
        return outs, state

    def halves(x):
        return [x[:, t * 2 * LANES:(t + 1) * 2 * LANES] for t in range(2)]

    def stage_mean(outs):
        return [[o_t - jnp.dot(o_t.astype(BF16), gn_ref[...], preferred_element_type=F32) for o_t in halves(o)]
                for o in outs]

    def stage_var(ds):
        return [[jnp.dot((d * d).astype(BF16), gn_ref[...], preferred_element_type=F32) for d in dv] for dv in ds]

    def stage_store(chunks, ds, vars_):
        for c, dv, vv in zip(chunks, ds, vars_):
            y = jnp.concatenate([d * lax.rsqrt(var + GN_EPS) for d, var in zip(dv, vv)], axis=1)
            g = rows(g_ref, c).astype(F32)
            o_ref[c * RET_CHUNK:(c + 1) * RET_CHUNK, :] = (y * (g * jax.nn.sigmoid(g))).astype(BF16)

    assert n_chunks % RET_GROUP == 0
    groups = [list(range(g0, g0 + RET_GROUP)) for g0 in range(0, n_chunks, RET_GROUP)]
    state = [jnp.zeros((pair_rows, LANES), F32) for _ in pairs]
    pending = None
    sc, kv = stage_scores(groups[0]), stage_kv(groups[0])
    for gi, chunks in enumerate(groups):
        inner = stage_pv(chunks, sc)
        if gi + 1 < len(groups):
            nxt = (stage_scores(groups[gi + 1]), stage_kv(groups[gi + 1]))
        outs, state = stage_recurrence(chunks, kv, inner, state)
        if pending is not None:
            stage_store(pending[0], pending[1], stage_var(pending[1]))
        pending = (chunks, stage_mean(outs))
        if gi + 1 < len(groups):
            sc, kv = nxt
    stage_store(pending[0], pending[1], stage_var(pending[1]))


def _retention(rq, rqd, rkt, rkdt, rv, rg, weights, batch, seq):
    tables = _retention_tables()
    consts = tuple(jnp.asarray(tables[n]) for n in ("inner_decay", "chunk_decay", "state_mask"))
    consts += (jnp.asarray(tables["group_mean"], BF16),)
    seq_spec = lambda w: pl.BlockSpec((seq, w), lambda b: (b, 0))
    t_spec = pl.BlockSpec((RET_QK_W, seq), lambda b: (0, b))
    w_specs = []
    for w in weights:
        rows = w.shape[0] // batch
        assert rows * batch == w.shape[0] and rows % BF16_SUBLANES == 0
        w_specs.append(pl.BlockSpec((rows, w.shape[1]), lambda b: (b, 0)))
    return pl.pallas_call(
        _retention_kernel,
        out_shape=(jax.ShapeDtypeStruct((batch * seq, RET_V_W), BF16),)
                  + tuple(jax.ShapeDtypeStruct(w.shape, BF16) for w in weights),
        grid=(batch,),
        in_specs=[seq_spec(RET_QK_W)] * 2 + [t_spec] * 2 + [seq_spec(RET_V_W)] * 2
                 + [_const_spec(c.shape) for c in consts] + w_specs,
        out_specs=(seq_spec(RET_V_W),) + tuple(w_specs),
        compiler_params=pltpu.CompilerParams(dimension_semantics=("arbitrary",),
                                             vmem_limit_bytes=VMEM_LIMIT),
        name="retention",
    )(rq, rqd, rkt, rkdt, rv, rg, *consts, *weights)


def _attend(qb, kb, vb, bias):
    nk = kb.shape[0]
    lane = lax.broadcasted_iota(jnp.int32, (ATT_BLOCK, LANES), 1)
    first_head = lane < ATT_DIM
    kbf = kb.astype(BF16)
    vext = jnp.concatenate([vb.astype(BF16), jnp.ones((nk, LANES), BF16)], axis=1)
    res = []
    for head_lanes in (first_head, jnp.logical_not(first_head)):
        qm = jnp.where(head_lanes, qb, 0.0).astype(BF16)
        s = lax.dot_general(qm, kbf, (((1,), (1,)), ((), ())), preferred_element_type=F32) + bias
        m = jnp.max(s, axis=-1, keepdims=True)
        p = jnp.exp2(s - m)
        pv = jnp.dot(p.astype(BF16), vext, preferred_element_type=F32)
        res.append((pv[:, :LANES], pv[:, LANES:], m))
    return tuple(jnp.where(first_head, a, b) for a, b in zip(*res))


def _attend_first(qb, kb, vb, bias2):
    lane = lax.broadcasted_iota(jnp.int32, (ATT_BLOCK, LANES), 1)
    first_head = lane < ATT_DIM
    sel_a = lambda x: jnp.where(first_head, x, 0.0).astype(BF16)
    sel_b = lambda x: jnp.where(first_head, 0.0, x).astype(BF16)
    one = jnp.ones((ATT_BLOCK, LANES), F32)
    kcat = jnp.concatenate([sel_a(kb), sel_b(kb)], axis=0)
    s = lax.dot_general(qb.astype(BF16), kcat, (((1,), (1,)), ((), ())), preferred_element_type=F32) + bias2
    ma = jnp.max(s[:, :ATT_BLOCK], axis=-1, keepdims=True)
    mb = jnp.max(s[:, ATT_BLOCK:], axis=-1, keepdims=True)
    p = jnp.concatenate([jnp.exp2(s[:, :ATT_BLOCK] - ma), jnp.exp2(s[:, ATT_BLOCK:] - mb)], axis=1)
    vcat = jnp.concatenate([jnp.concatenate([sel_a(vb), sel_a(one)], axis=1),
                            jnp.concatenate([sel_b(vb), sel_b(one)], axis=1)], axis=0)
    pv = jnp.dot(p.astype(BF16), vcat, preferred_element_type=F32)
    return pv[:, :LANES], pv[:, LANES:], jnp.where(first_head, ma, mb)


def _dilated_kernel(q_ref, k_ref, v_ref, o_ref, bias_ref, bias2_ref, q4_ref, k4_ref, v4_ref, *scratch):
    seq = q_ref.shape[0]
    blk = ATT_BLOCK
    sd = STAGE_DIL
    lc = seq // sd
    a = lax.broadcasted_iota(jnp.int32, (blk, 2 * blk), 0)
    kk = lax.broadcasted_iota(jnp.int32, (blk, 2 * blk), 1)
    dist = blk + a - kk
    bias_ref[...] = jnp.where((dist >= 0) & (dist <= blk), 0.0, -jnp.inf)
    causal = jnp.where(kk % blk <= a, 0.0, -jnp.inf)
    bias2_ref[...] = causal
    for r in range(sd):
        for src, dst in ((q_ref, q4_ref), (k_ref, k4_ref), (v_ref, v4_ref)):
            dst[r * lc:(r + 1) * lc, :] = src[pl.ds(r, lc, stride=sd), :]

    def view(p_idx, cls, b, n):
        dil = DILATED_PATTERNS[p_idx][1]
        if dil == 1:
            return (q_ref, k_ref, v_ref), pl.ds(b * blk, n)
        if dil == sd:
            return (q4_ref, k4_ref, v4_ref), pl.ds(cls * lc + b * blk, n)
        return (q4_ref, k4_ref, v4_ref), pl.ds((cls % sd) * lc + cls // sd + b * blk * sd, n, stride=sd)

    def run(tasks, first):
        loaded = []
        for p_idx, cls, b in tasks:
            (qr, kr, vr), qs = view(p_idx, cls, b, blk)
            _, ks = (None, qs) if first else view(p_idx, cls, b - 1, 2 * blk)
            loaded.append((qr[qs, :], kr[ks, :], vr[ks, :], qs))
        if first:
            bias2 = bias2_ref[...]
            results = [_attend_first(qb, kb, vb, bias2) for qb, kb, vb, _ in loaded]
        else:
            bias = bias_ref[...]
            results = [_attend(qb, kb, vb, bias) for qb, kb, vb, _ in loaded]
        for (p_idx, _, _), (_, _, _, dst), res in zip(tasks, loaded, results):
            for ref, val in zip(scratch[3 * p_idx:3 * p_idx + 3], res):
                ref[dst, :] = val

    n_blocks = [seq // dil // blk for _, dil in DILATED_PATTERNS]
    run([(p, cls, 0) for p, (_, dil) in enumerate(DILATED_PATTERNS) for cls in range(dil)], True)
    for p, (_, dil) in enumerate(DILATED_PATTERNS):
        if n_blocks[p] > 1:
            run([(p, cls, b) for b in range(1, n_blocks[p]) for cls in range(dil)], False)

    def combine(c, carry):
        for r in range(sd):
            nat = pl.ds(r + c * blk * sd, blk, stride=sd)
            cls = pl.ds(pl.multiple_of(r * lc + c * blk, blk), blk)
            sls = [nat if dil == 1 else cls for _, dil in DILATED_PATTERNS]
            ms = [scratch[3 * p + 2][sl, :] for p, sl in enumerate(sls)]
            m = functools.reduce(jnp.maximum, ms)
            ws = [jnp.exp2(mp - m) for mp in ms]
            num = functools.reduce(lambda x, y: x + y,
                                   [w * scratch[3 * p][sl, :] for p, (w, sl) in enumerate(zip(ws, sls))])
            den = functools.reduce(lambda x, y: x + y,
                                   [w * scratch[3 * p + 1][sl, :] for p, (w, sl) in enumerate(zip(ws, sls))])
            o_ref[nat, :] = num / den
        return carry

    lax.fori_loop(0, lc // blk, combine, 0)


def _dilated(aq, ak, av, batch, seq):
    dils = [dil for _, dil in DILATED_PATTERNS]
    assert dils == [1, STAGE_DIL, STAGE_DIL * STAGE_DIL]
    for window, dil in DILATED_PATTERNS:
        assert window // dil == ATT_BLOCK and (seq // dil) % ATT_BLOCK == 0
    spec = pl.BlockSpec((seq, LANES), lambda b, hp: (b, hp))
    return pl.pallas_call(
        _dilated_kernel,
        out_shape=jax.ShapeDtypeStruct((batch * seq, ATT_W), F32),
        grid=(batch, ATT_W // LANES),
        in_specs=[spec, spec, spec],
        out_specs=spec,
        scratch_shapes=[pltpu.VMEM((ATT_BLOCK, 2 * ATT_BLOCK), F32) for _ in range(2)]
                       + [pltpu.VMEM((seq, LANES), F32) for _ in range(3 + 3 * len(DILATED_PATTERNS))],
        compiler_params=pltpu.CompilerParams(dimension_semantics=("arbitrary", "arbitrary"),
                                             vmem_limit_bytes=VMEM_LIMIT),
        name="dilated",
    )(aq, ak, av)


def _out_ffn_kernel(ret_ref, att_ref, x_ref, wo_ref, g1_ref, g2_ref, g3_ref, wg_ref, wu_ref, wd_ref,
                    o_ref):
    n_sub = x_ref.shape[0] // FFN_SUB_ROWS

    def head(s):
        rows = pl.ds(s * FFN_SUB_ROWS, FFN_SUB_ROWS)
        mix = (jnp.dot(ret_ref[rows, :], wo_ref[:RET_V_W, :], preferred_element_type=F32)
               + jnp.dot(att_ref[rows, :].astype(BF16), wo_ref[RET_V_W:, :], preferred_element_type=F32))
        x1 = x_ref[rows, :] + _rms(mix, g1_ref[...])
        return x1, _rms(x1, g2_ref[...]).astype(BF16)

    def swiglu(h):
        acc = None
        for c in range(N_FF_CHUNKS):
            cols = slice(c * FF_CHUNK, (c + 1) * FF_CHUNK)
            a = jnp.dot(h, wg_ref[:, cols], preferred_element_type=F32)
            u = jnp.dot(h, wu_ref[:, cols], preferred_element_type=F32)
            act = (a * jax.nn.sigmoid(a) * u).astype(BF16)
            d = jnp.dot(act, wd_ref[cols, :], preferred_element_type=F32)
            acc = d if acc is None else acc + d
        return acc

    cur = head(0)
    for s in range(n_sub):
        nxt = head(s + 1) if s + 1 < n_sub else None
        x1, h = cur
        o_ref[pl.ds(s * FFN_SUB_ROWS, FFN_SUB_ROWS), :] = x1 + _rms(swiglu(h), g3_ref[...])
        cur = nxt


def _out_ffn(ret, att, x2, wo, g1, g2, g3, wg, wu, wd, tm):
    t = x2.shape[0]
    row = lambda w: pl.BlockSpec((tm, w), lambda i: (i, 0))
    consts = (wo, g1, g2, g3, wg, wu, wd)
    return pl.pallas_call(
        _out_ffn_kernel,
        out_shape=jax.ShapeDtypeStruct((t, D_MODEL), F32),
        grid=(t // tm,),
        in_specs=[row(RET_V_W), row(ATT_W), row(D_MODEL)] + [_const_spec(c.shape) for c in consts],
        out_specs=row(D_MODEL),
        compiler_params=pltpu.CompilerParams(dimension_semantics=("arbitrary",),
                                             vmem_limit_bytes=VMEM_LIMIT),
        name="out_ffn",
    )(ret, att, x2, *consts)


def _layer(x2, pos2, batch, seq, w_in, w_out, g_pre_mix, g_post_mix, g_pre_ffn, g_post_ffn,
           w_gate, w_up, w_down):
    row = lambda g: g.reshape(1, D_MODEL).astype(F32)
    rq, rqd, rkt, rkdt, rv, rg, aq, ak, av = _in_proj(x2, pos2, row(g_pre_mix), w_in, IN_PROJ_ROWS)
    ret, wo_bf, wg_bf, wu_bf, wd_bf = _retention(rq, rqd, rkt, rkdt, rv, rg, (w_out, w_gate, w_up, w_down),
                                                 batch, seq)
    att = _dilated(aq, ak, av, batch, seq)
    return _out_ffn(ret, att, x2, wo_bf, row(g_post_mix), row(g_pre_ffn), row(g_post_ffn),
                    wg_bf, wu_bf, wd_bf, FFN_ROWS)


def kernel(x, positions, w_in, w_out, g_pre_mix, g_post_mix, g_pre_ffn, g_post_ffn, w_gate, w_up, w_down):
    batch, seq, d = x.shape
    assert d == D_MODEL and seq % (ATT_BLOCK * DILATED_PATTERNS[-1][1]) == 0
    assert (batch * seq) % FFN_ROWS == 0 and (batch * seq) % IN_PROJ_ROWS == 0
    x2 = x.reshape(batch * seq, d)
    pos2 = positions.reshape(batch * seq // LANES, LANES)
    for l in range(w_in.shape[0]):
        x2 = _layer(x2, pos2, batch, seq, w_in[l], w_out[l], g_pre_mix[l], g_post_mix[l], g_pre_ffn[l],
                    g_post_ffn[l], w_gate[l], w_up[l], w_down[l])
    return x2.reshape(batch, seq, d)
```
